```python
import math
import jax, jax.numpy as jnp
from jax import lax
import numpy as np

D_MODEL = 1024
BATCH = 8
SEQ = 4096
DEPTH = 2

GRID_W = 64
CTX_LEN = 256

MLA_HEADS = 8
QK_NOPE = 64
QK_ROPE = 32
V_HEAD = 64
Q_LORA = 384
KV_LORA = 256
AXIS_ROPE = QK_ROPE // 2
ROPE_BASE = 10000.0
Q_BLOCK = 128

FNET_GROUPS = 4
FNET_GROUP_W = 64
FNET_W = FNET_GROUPS * FNET_GROUP_W

HYENA_W = 256
HYENA_CONV = 3
FILTER_EMB = 33
FILTER_ORDER = 64
DECAY_TARGET = 1e-2
FAST_DECAY_PCT = 0.3
SLOW_DECAY_PCT = 1.5

MLA_W = MLA_HEADS * V_HEAD
MIX_W = MLA_W + FNET_W + HYENA_W
OFF_Q = 0
OFF_KV = OFF_Q + Q_LORA
OFF_KR = OFF_KV + KV_LORA
OFF_F = OFF_KR + QK_ROPE
OFF_H = OFF_F + FNET_W
IN_W = OFF_H + 3 * HYENA_W
D_FF = 4 * D_MODEL
N_MOD = 6
EPS = 1e-6

kernel_name = "hymba_mla_fnet_hyena_dit_trunk"


def rms_norm(x, g):
    xf = x.astype(jnp.float32)
    y = xf * lax.rsqrt(jnp.mean(xf * xf, axis=-1, keepdims=True) + EPS)
    return (y * g.astype(jnp.float32)).astype(x.dtype)


def modulation(cvec, w_mod, b_mod):
    m = (jax.nn.silu(cvec) @ w_mod + b_mod).reshape(-1, 1, N_MOD * D_MODEL)
    return jnp.split(m, N_MOD, axis=-1)


def modulate(h, shift, scale):
    return h * (1.0 + scale) + shift


def grid_rope_tables(n_tokens, dtype):
    rows = n_tokens // GRID_W
    row = jnp.repeat(jnp.arange(rows, dtype=jnp.float32), GRID_W)
    col = jnp.tile(jnp.arange(GRID_W, dtype=jnp.float32), rows)
    inv = ROPE_BASE ** (-jnp.arange(0, AXIS_ROPE, 2, dtype=jnp.float32) / AXIS_ROPE)
    ang = jnp.concatenate([row[:, None] * inv, col[:, None] * inv], axis=-1)
    return jnp.cos(ang).astype(dtype), jnp.sin(ang).astype(dtype)


def _rotate_half(v, cos, sin):
    half = v.shape[-1] // 2
    v1, v2 = v[..., :half], v[..., half:]
    return jnp.concatenate([v1 * cos - v2 * sin, v1 * sin + v2 * cos], axis=-1)


def apply_axial_rope(x, cos, sin):
    n = AXIS_ROPE // 2
    return jnp.concatenate([
        _rotate_half(x[..., :AXIS_ROPE], cos[..., :n], sin[..., :n]),
        _rotate_half(x[..., AXIS_ROPE:], cos[..., n:], sin[..., n:])], axis=-1)


def mla_queries(p, q_norm_g, w_uq, rope):
    B, L, _ = p.shape
    cq = rms_norm(p[..., OFF_Q:OFF_KV], q_norm_g)
    q = (cq @ w_uq).reshape(B, L, MLA_HEADS, QK_NOPE + QK_ROPE)
    q_nope, q_rope = q[..., :QK_NOPE], q[..., QK_NOPE:]
    if rope is not None:
        cos, sin = rope
        q_rope = apply_axial_rope(q_rope, cos[:, None, :], sin[:, None, :])
    return q_nope, q_rope


def mla_keys_values(p_kv, kv_norm_g, w_ukv, rope):
    B, L, _ = p_kv.shape
    ckv = rms_norm(p_kv[..., :KV_LORA], kv_norm_g)
    kv = (ckv @ w_ukv).reshape(B, L, MLA_HEADS, QK_NOPE + V_HEAD)
    k_rope = p_kv[..., KV_LORA:]
    if rope is not None:
        k_rope = apply_axial_rope(k_rope, *rope)
    return kv[..., :QK_NOPE], k_rope, kv[..., QK_NOPE:]


def mla_attention(q_nope, q_rope, k_nope, k_rope, v):
    B, L, H, _ = q_nope.shape
    nb = L // Q_BLOCK
    scale = 1.0 / math.sqrt(QK_NOPE + QK_ROPE)

    def to_blocks(t):
        return jnp.moveaxis(t.reshape(B, nb, Q_BLOCK, *t.shape[2:]), 1, 0)

    def block(qs):
        qn, qr = qs
        s = (jnp.einsum('bqhd,bkhd->bhqk', qn, k_nope)
             + jnp.einsum('bqhr,bkr->bhqk', qr, k_rope))
        prob = jax.nn.softmax(s.astype(jnp.float32) * scale, axis=-1).astype(v.dtype)
        return jnp.einsum('bhqk,bkhd->bqhd', prob, v)

    o = lax.map(block, (to_blocks(q_nope), to_blocks(q_rope)))
    return jnp.moveaxis(o, 0, 1).reshape(B, L, H * V_HEAD)


def fourier_mix(u):
    B, L, _ = u.shape
    g = u.reshape(B, L, FNET_GROUPS, FNET_GROUP_W).astype(jnp.float32)
    f = jnp.fft.fftn(g, axes=(1, 3), norm="ortho").real
    return f.reshape(B, L, FNET_W).astype(u.dtype)


def short_conv(u, w, b):
    L = u.shape[1]
    pad = (HYENA_CONV - 1) // 2
    up = jnp.pad(u, ((0, 0), (pad, HYENA_CONV - 1 - pad), (0, 0)))
    out = b
    for k in range(HYENA_CONV):
        out = out + up[:, k:k + L] * w[k]
    return out


def hyena_filters(L, w1, b1, freq, w2, b2, w3):
    f32 = jnp.float32
    t = jnp.linspace(0.0, 1.0, L, dtype=f32)[:, None]
    bands = (FILTER_EMB - 1) // 2
    fr = jnp.linspace(1e-4, bands - 1, bands, dtype=f32)
    ang = 2.0 * math.pi * jnp.arange(L, dtype=f32)[:, None] / L * fr
    z = jnp.concatenate([t, jnp.cos(ang), -jnp.sin(ang)], axis=-1)
    h = jnp.sin(freq * (z.astype(w1.dtype) @ w1 + b1))
    h = jnp.sin(freq * (h @ w2 + b2))
    h = (h @ w3).astype(f32)
    min_decay = math.log(DECAY_TARGET) / SLOW_DECAY_PCT
    max_decay = math.log(DECAY_TARGET) / FAST_DECAY_PCT
    deltas = jnp.abs(jnp.linspace(min_decay, max_decay, HYENA_W, dtype=f32))
    decay = jnp.exp(-t * deltas)
    h = h * jnp.tile(decay, (1, 2))
    h_fwd, h_bwd = h[:, :HYENA_W], h[:, HYENA_W:]
    k = jnp.concatenate([h_fwd, jnp.zeros((1, HYENA_W), f32), h_bwd[:0:-1]], axis=0)
    return k / jnp.sum(jnp.abs(k), axis=0, keepdims=True)


def hyena_mix(u, conv_w, conv_b, w1, b1, freq, w2, b2, w3, d_bias):
    B, L, _ = u.shape
    u = short_conv(u, conv_w, conv_b)
    x0, x1, v = jnp.split(u, 3, axis=-1)
    z = (x1 * v).astype(jnp.float32)
    k = hyena_filters(L, w1, b1, freq, w2, b2, w3)
    zf = jnp.fft.rfft(z, n=2 * L, axis=1)
    kf = jnp.fft.rfft(k, n=2 * L, axis=0)
    y = jnp.fft.irfft(zf * kf[None], n=2 * L, axis=1)[:, :L] + d_bias.astype(jnp.float32) * z
    return (x0.astype(jnp.float32) * y).astype(u.dtype)


def heads_out(att, p, hy, w_out):
    f = fourier_mix(p[..., OFF_F:OFF_H])
    hz = hyena_mix(p[..., OFF_H:], *hy)
    return jnp.concatenate([att, f, hz], axis=-1) @ w_out


def sq_relu_mlp(h, w1, w2):
    return jnp.square(jax.nn.relu(h @ w1)) @ w2


def setup_inputs(seed: int = 0) -> dict:
    key = jax.random.key(seed)
    ks = jax.random.split(key, 26)

    def nrm(k, shape, scale):
        return jax.random.normal(k, shape, jnp.float32) * scale

    D = D_MODEL
    return {
        "x": nrm(ks[0], (BATCH, SEQ, D), 1.0),
        "c": nrm(ks[1], (BATCH, D), 1.0),
        "ctx": nrm(ks[2], (BATCH, CTX_LEN, D), 1.0),
        "c_ctx": nrm(ks[3], (D,), 1.0),
        "norm1_g": 1.0 + nrm(ks[4], (DEPTH, D), 0.05),
        "norm2_g": 1.0 + nrm(ks[5], (DEPTH, D), 0.05),
        "w_mod": nrm(ks[6], (DEPTH, D, N_MOD * D), 0.5 * D ** -0.5),
        "b_mod": nrm(ks[7], (DEPTH, N_MOD * D), 0.01),
        "w_in": nrm(ks[8], (DEPTH, D, IN_W), D ** -0.5),
        "q_norm_g": 1.0 + nrm(ks[9], (DEPTH, Q_LORA), 0.05),
        "kv_norm_g": 1.0 + nrm(ks[10], (DEPTH, KV_LORA), 0.05),
        "w_uq": nrm(ks[11], (DEPTH, Q_LORA, MLA_HEADS * (QK_NOPE + QK_ROPE)), Q_LORA ** -0.5),
        "w_ukv": nrm(ks[12], (DEPTH, KV_LORA, MLA_HEADS * (QK_NOPE + V_HEAD)), KV_LORA ** -0.5),
        "hy_conv_w": nrm(ks[13], (DEPTH, HYENA_CONV, 3 * HYENA_W), HYENA_CONV ** -0.5),
        "hy_conv_b": nrm(ks[14], (DEPTH, 3 * HYENA_W), 0.01),
        "hy_w1": nrm(ks[15], (DEPTH, FILTER_EMB, FILTER_ORDER), FILTER_EMB ** -0.5),
        "hy_b1": nrm(ks[16], (DEPTH, FILTER_ORDER), 0.1),
        "hy_freq": 1.0 + nrm(ks[17], (DEPTH, FILTER_ORDER), 0.05),
        "hy_w2": nrm(ks[18], (DEPTH, FILTER_ORDER, FILTER_ORDER), FILTER_ORDER ** -0.5),
        "hy_b2": nrm(ks[19], (DEPTH, FILTER_ORDER), 0.1),
        "hy_w3": nrm(ks[20], (DEPTH, FILTER_ORDER, 2 * HYENA_W), FILTER_ORDER ** -0.5),
        "hy_d": nrm(ks[21], (DEPTH, HYENA_W), 0.5),
        "w_out": nrm(ks[22], (DEPTH, MIX_W, D), MIX_W ** -0.5),
        "w_mlp1": nrm(ks[23], (DEPTH, D, D_FF), D ** -0.5),
        "w_mlp2": nrm(ks[24], (DEPTH, D_FF, D), D_FF ** -0.5),
        "final_norm_g": 1.0 + nrm(ks[25], (D,), 0.05),
    }


def reference(x, c, ctx, c_ctx, norm1_g, norm2_g, w_mod, b_mod, w_in, q_norm_g, kv_norm_g,
              w_uq, w_ukv, hy_conv_w, hy_conv_b, hy_w1, hy_b1, hy_freq, hy_w2, hy_b2, hy_w3,
              hy_d, w_out, w_mlp1, w_mlp2, final_norm_g):
    L = x.shape[1]
    rope = grid_rope_tables(L, x.dtype)
    xc = ctx
    for l in range(DEPTH):
        last = l == DEPTH - 1
        hy = (hy_conv_w[l], hy_conv_b[l], hy_w1[l], hy_b1[l], hy_freq[l],
              hy_w2[l], hy_b2[l], hy_w3[l], hy_d[l])
        sh1, sc1, g1, sh2, sc2, g2 = modulation(c, w_mod[l], b_mod[l])
        csh1, csc1, cg1, csh2, csc2, cg2 = modulation(c_ctx, w_mod[l], b_mod[l])

        hx = modulate(rms_norm(x, norm1_g[l]), sh1, sc1)
        hc = modulate(rms_norm(xc, norm1_g[l]), csh1, csc1)
        px = hx @ w_in[l]
        if last:
            pc_kv = hc @ w_in[l][:, OFF_KV:OFF_F]
        else:
            pc = hc @ w_in[l]
            pc_kv = pc[..., OFF_KV:OFF_F]
        kn_c, kr_c, v_c = mla_keys_values(pc_kv, kv_norm_g[l], w_ukv[l], None)
        kn_x, kr_x, v_x = mla_keys_values(px[..., OFF_KV:OFF_F], kv_norm_g[l], w_ukv[l], rope)
        qn_x, qr_x = mla_queries(px, q_norm_g[l], w_uq[l], rope)
        att_x = mla_attention(qn_x, qr_x,
                              jnp.concatenate([kn_x, kn_c], axis=1),
                              jnp.concatenate([kr_x, kr_c], axis=1),
                              jnp.concatenate([v_x, v_c], axis=1))
        x = x + g1 * heads_out(att_x, px, hy, w_out[l])

        x = x + g2 * sq_relu_mlp(modulate(rms_norm(x, norm2_g[l]), sh2, sc2), w_mlp1[l], w_mlp2[l])

        if not last:
            qn_c, qr_c = mla_queries(pc, q_norm_g[l], w_uq[l], None)
            att_c = mla_attention(qn_c, qr_c, kn_c, kr_c, v_c)
            xc = xc + cg1 * heads_out(att_c, pc, hy, w_out[l])
            xc = xc + cg2 * sq_relu_mlp(modulate(rms_norm(xc, norm2_g[l]), csh2, csc2),
                                        w_mlp1[l], w_mlp2[l])
    return rms_norm(x, final_norm_g)
```

```python
import functools
import math

import jax
import jax.numpy as jnp
from jax import lax
from jax.experimental import pallas as pl
from jax.experimental.pallas import tpu as pltpu

F32 = jnp.float32
MXU_DTYPE = jnp.bfloat16

HEADS = 8
QK_NOPE_W = 64
QK_ROPE_W = 32
V_HEAD_W = 64
Q_LORA_W = 384
KV_LORA_W = 256
AXIS_ROPE_W = QK_ROPE_W // 2
ROPE_THETA = 10000.0
ROPE_GRID_W = 64
FNET_GROUP = 64
FNET_WIDTH = 256
HYENA_WIDTH = 256
N_MODULATION = 6
NORM_EPS = 1e-6
DECAY_TARGET_VAL = 1e-2
FAST_DECAY = 0.3
SLOW_DECAY = 1.5

HEAD_SLAB = 128
OFF_KV_LORA = Q_LORA_W
OFF_K_ROPE = OFF_KV_LORA + KV_LORA_W
OFF_FNET = OFF_K_ROPE + QK_ROPE_W
OFF_HYENA = OFF_FNET + FNET_WIDTH

AUG_KR = OFF_K_ROPE
AUG_KR_SWAP = AUG_KR + HEAD_SLAB
AUG_F = AUG_KR_SWAP + HEAD_SLAB
AUG_H = AUG_F + FNET_WIDTH
AUG_W = AUG_H + 3 * HYENA_WIDTH

V7X_VMEM_LIMIT = 56 * 1024 * 1024


def _cparams(sem):
    return pltpu.CompilerParams(dimension_semantics=sem, vmem_limit_bytes=V7X_VMEM_LIMIT)


def _dot(a, b):
    return jnp.dot(a, b, preferred_element_type=F32)


def _rms(x, g):
    return x * lax.rsqrt(jnp.mean(x * x, axis=-1, keepdims=True) + NORM_EPS) * g


def _const_spec(shape):
    n = len(shape)
    return pl.BlockSpec(shape, lambda *_: (0,) * n)


def _mod_kernel(c_ref, w_ref, b_ref, o_ref):
    c = c_ref[...]
    s = c / (1.0 + jnp.exp(-c))
    o_ref[0] = _dot(s.astype(MXU_DTYPE), w_ref[0].astype(MXU_DTYPE)) + b_ref[0]


def _modulation(cc, w_mod, b_mod):
    depth, d, n = w_mod.shape
    r = cc.shape[0]
    tn = 1024
    return pl.pallas_call(
        _mod_kernel,
        out_shape=jax.ShapeDtypeStruct((depth, r, n), F32),
        grid=(depth, n // tn),
        in_specs=[pl.BlockSpec((r, d), lambda l, j: (0, 0)),
                  pl.BlockSpec((1, d, tn), lambda l, j: (l, 0, j)),
                  pl.BlockSpec((1, 1, tn), lambda l, j: (l, 0, j))],
        out_specs=pl.BlockSpec((1, r, tn), lambda l, j: (l, 0, j)),
        compiler_params=_cparams(("arbitrary", "arbitrary")),
        name="modulation",
    )(cc, w_mod, b_mod.reshape(depth, 1, n))


def _in_proj_kernel(x_ref, sh_ref, sc_ref, g_ref, win_ref, qg_ref, kvg_ref, wqa_ref, wqb_ref,
                    wk_ref, wv_ref, cos_ref, sin_ref, q_ref, k_ref, v_ref, f_ref, u_ref, *, q_scale):
    h = _rms(x_ref[0], g_ref[...]) * (1.0 + sc_ref[0]) + sh_ref[0]
    p = _dot(h.astype(MXU_DTYPE), win_ref[...])
    cos = cos_ref[...]
    sin = sin_ref[...]
    cq = _rms(p[:, :Q_LORA_W], qg_ref[...]).astype(MXU_DTYPE)
    qa = _dot(cq, wqa_ref[...])
    qb = _dot(cq, wqb_ref[...])
    ckv = _rms(p[:, OFF_KV_LORA:OFF_K_ROPE], kvg_ref[...]).astype(MXU_DTYPE)
    kn = _dot(ckv, wk_ref[...])
    kr = p[:, AUG_KR:AUG_KR_SWAP] * cos + p[:, AUG_KR_SWAP:AUG_F] * sin
    for hd in range(HEADS):
        hs = slice(hd * HEAD_SLAB, (hd + 1) * HEAD_SLAB)
        q_ref[0, :, hs] = ((qa[:, hs] * cos + qb[:, hs] * sin) * q_scale).astype(q_ref.dtype)
        k_ref[0, :, hs] = (kn[:, hs] + kr).astype(k_ref.dtype)
    v_ref[0] = _dot(ckv, wv_ref[...]).astype(v_ref.dtype)
    f_ref[0] = p[:, AUG_F:AUG_H].astype(f_ref.dtype)
    u_ref[0] = p[:, AUG_H:AUG_W]


def _in_proj(x, shift, scale, norm_g, lw, cos_t, sin_t):
    b, l, d = x.shape
    tm = min(512, l)
    q_scale = math.log2(math.e) / math.sqrt(QK_NOPE_W + QK_ROPE_W)
    tok = lambda w: pl.BlockSpec((1, tm, w), lambda bi, i: (bi, i, 0))
    vec = pl.BlockSpec((1, 1, d), lambda bi, i: (bi, 0, 0))
    tab = pl.BlockSpec((tm, HEAD_SLAB), lambda bi, i: (i, 0))
    qk_w = HEADS * HEAD_SLAB
    v_w = HEADS * V_HEAD_W
    return pl.pallas_call(
        functools.partial(_in_proj_kernel, q_scale=q_scale),
        out_shape=(jax.ShapeDtypeStruct((b, l, qk_w), MXU_DTYPE),
                   jax.ShapeDtypeStruct((b, l, qk_w), MXU_DTYPE),
                   jax.ShapeDtypeStruct((b, l, v_w), MXU_DTYPE),
                   jax.ShapeDtypeStruct((b, l, FNET_WIDTH), MXU_DTYPE),
                   jax.ShapeDtypeStruct((b, l, 3 * HYENA_WIDTH), F32)),
        grid=(b, l // tm),
        in_specs=[tok(d), vec, vec, _const_spec((1, d)), _const_spec((d, AUG_W)),
                  _const_spec((1, Q_LORA_W)), _const_spec((1, KV_LORA_W)),
                  _const_spec((Q_LORA_W, qk_w)), _const_spec((Q_LORA_W, qk_w)),
                  _const_spec((KV_LORA_W, qk_w)), _const_spec((KV_LORA_W, v_w)), tab, tab],
        out_specs=(tok(qk_w), tok(qk_w), tok(v_w), tok(FNET_WIDTH), tok(3 * HYENA_WIDTH)),
        compiler_params=_cparams(("parallel", "arbitrary")),
        name="in_proj",
    )(x, shift, scale, norm_g, lw["w_in"], lw["q_g"], lw["kv_g"], lw["wq_a"], lw["wq_b"],
      lw["wk"], lw["wv"], cos_t, sin_t)


def _attn_kernel(*refs, nseg, tk):
    q_ref, o_ref = refs[0], refs[-1]
    segs = [(refs[1 + 2 * s], refs[2 + 2 * s]) for s in range(nseg)]
    tq = q_ref.shape[1]
    lane = lax.broadcasted_iota(jnp.int32, (1, HEAD_SLAB), 1)
    first_half = lane < V_HEAD_W
    out = jnp.zeros((tq, HEAD_SLAB), F32)
    for hd in range(2):
        qh = q_ref[0, :, hd * HEAD_SLAB:(hd + 1) * HEAD_SLAB]
        keep = first_half if hd == 0 else jnp.logical_not(first_half)
        m = jnp.full((tq, 1), -jnp.inf, F32)
        l = jnp.zeros((tq, 1), F32)
        acc = jnp.zeros((tq, HEAD_SLAB), F32)
        for k_ref, v_ref in segs:
            s_len = k_ref.shape[1]
            ck = min(tk, s_len)
            for c in range(s_len // ck):
                kc = k_ref[0, c * ck:(c + 1) * ck, hd * HEAD_SLAB:(hd + 1) * HEAD_SLAB]
                vc = v_ref[0, c * ck:(c + 1) * ck, :]
                vc = jnp.where(keep, vc, jnp.zeros_like(vc))
                s = lax.dot_general(qh, kc, (((1,), (1,)), ((), ())), preferred_element_type=F32)
                m_new = jnp.maximum(m, jnp.max(s, axis=-1, keepdims=True))
                alpha = jnp.exp2(m - m_new)
                p = jnp.exp2(s - m_new)
                l = alpha * l + jnp.sum(p, axis=-1, keepdims=True)
                acc = alpha * acc + _dot(p.astype(MXU_DTYPE), vc)
                m = m_new
        out = out + acc / l
    o_ref[0] = out.astype(o_ref.dtype)


def _attention(q, kv_segs):
    b, l, _ = q.shape
    tq = min(256, l)
    pair_qk = 2 * HEAD_SLAB
    pair_v = 2 * V_HEAD_W
    in_specs = [pl.BlockSpec((1, tq, pair_qk), lambda bi, hp, i: (bi, i, hp))]
    args = [q]
    for k, v in kv_segs:
        s_len = k.shape[1]
        in_specs.append(pl.BlockSpec((1, s_len, pair_qk), lambda bi, hp, i: (bi, 0, hp)))
        in_specs.append(pl.BlockSpec((1, s_len, pair_v), lambda bi, hp, i: (bi, 0, hp)))
        args += [k, v]
    return pl.pallas_call(
        functools.partial(_attn_kernel, nseg=len(kv_segs), tk=512),
        out_shape=jax.ShapeDtypeStruct((b, l, HEADS * V_HEAD_W), MXU_DTYPE),
        grid=(b, HEADS // 2, l // tq),
        in_specs=in_specs,
        out_specs=pl.BlockSpec((1, tq, pair_v), lambda bi, hp, i: (bi, i, hp)),
        compiler_params=_cparams(("parallel", "parallel", "arbitrary")),
        name="attention",
    )(*args)


def _split(n, zero_padded):
    n1 = 1 << (int(math.log2(n)) // 2)
    n2 = n // n1
    if zero_padded and n1 // 2 < 16 and n2 > n1:
        n1, n2 = n2, n1
    return n1, n2


def _angles(rows, cols, n):
    idx = (rows[:, None] * cols[None, :]) % n
    return idx.astype(F32) * (2.0 * math.pi / n)


def _coarse_fwd_table(n1, k_in):
    th = _angles(jnp.arange(n1, dtype=jnp.int32), jnp.arange(k_in, dtype=jnp.int32), n1)
    return jnp.concatenate([jnp.cos(th), -jnp.sin(th)], axis=0)


def _coarse_inv_table(n1, t_out, n):
    th = _angles(jnp.arange(t_out, dtype=jnp.int32), jnp.arange(n1, dtype=jnp.int32), n1)
    return jnp.concatenate([jnp.cos(th), -jnp.sin(th)], axis=1) * (1.0 / n)


def _fine_tables(n1, n2):
    n = n1 * n2
    k = (jnp.arange(n1, dtype=jnp.int32)[:, None] + n1 * jnp.arange(n2, dtype=jnp.int32)[None, :])
    idx = (k[:, :, None] * jnp.arange(n2, dtype=jnp.int32)[None, None, :]) % n
    ph = idx.astype(F32) * (2.0 * math.pi / n)
    c, s = jnp.cos(ph), jnp.sin(ph)
    fwd = jnp.concatenate([jnp.concatenate([c, s], axis=2), jnp.concatenate([-s, c], axis=2)], axis=1)
    return fwd, jnp.swapaxes(fwd, 1, 2)


def _lmm_kernel(t_ref, x_ref, o_ref):
    o_ref[0] = _dot(t_ref[...], x_ref[0].astype(MXU_DTYPE)).astype(o_ref.dtype)


def _left_matmul(table, x, out_dtype):
    b, k, n = x.shape
    m = table.shape[0]
    tc = min(2048, n)
    return pl.pallas_call(
        _lmm_kernel,
        out_shape=jax.ShapeDtypeStruct((b, m, n), out_dtype),
        grid=(b, n // tc),
        in_specs=[_const_spec((m, k)), pl.BlockSpec((1, k, tc), lambda bi, j: (bi, 0, j))],
        out_specs=pl.BlockSpec((1, m, tc), lambda bi, j: (bi, 0, j)),
        compiler_params=_cparams(("parallel", "arbitrary")),
        name="dft_coarse",
    )(table.astype(MXU_DTYPE), x)


def _fine_fwd(x_ref, g_ref, j):
    n2 = x_ref.shape[3]
    x = x_ref[0, :, j].reshape(2 * n2, x_ref.shape[4])
    return _dot(g_ref[j], x)


def _hy_mid_kernel(x_ref, g_ref, h_ref, kf_ref, o_ref):
    n2 = x_ref.shape[3]
    for j in range(x_ref.shape[2]):
        zf = _fine_fwd(x_ref, g_ref, j)
        zr, zi = zf[:n2], zf[n2:]
        kr, ki = kf_ref[j, :n2], kf_ref[j, n2:]
        y = jnp.concatenate([zr * kr - zi * ki, zr * ki + zi * kr], axis=0).astype(MXU_DTYPE)
        bk = _dot(h_ref[j], y)
        o_ref[0, 0, j] = bk[:n2].astype(o_ref.dtype)
        o_ref[0, 1, j] = bk[n2:].astype(o_ref.dtype)


def _kf_mid_kernel(x_ref, g_ref, o_ref):
    n2 = x_ref.shape[3]
    c = o_ref.shape[2]
    for j in range(x_ref.shape[2]):
        zf = _fine_fwd(x_ref, g_ref, j)
        o_ref[j, :n2] = zf[:n2, :c] + zf[:n2, c:]
        o_ref[j, n2:] = zf[n2:, :c] - zf[n2:, c:]


def _fn_mid_kernel(x_ref, g_ref, cm_ref, o_ref):
    for j in range(x_ref.shape[2]):
        pf = _fine_fwd(x_ref, g_ref, j).astype(MXU_DTYPE)
        n2 = x_ref.shape[3]
        o_ref[0, j] = (_dot(pf[:n2], cm_ref[0]) + _dot(pf[n2:], cm_ref[1])).astype(o_ref.dtype)


def _k1_tile(n1):
    return min(8, n1)


def _hyena_mid(a5, g, h, kf):
    b, _, n1, n2, c = a5.shape
    t1 = _k1_tile(n1)
    blk = pl.BlockSpec((1, 2, t1, n2, c), lambda bi, i: (bi, 0, i, 0, 0))
    mat = pl.BlockSpec((t1, 2 * n2, 2 * n2), lambda bi, i: (i, 0, 0))
    return pl.pallas_call(
        _hy_mid_kernel,
        out_shape=jax.ShapeDtypeStruct(a5.shape, MXU_DTYPE),
        grid=(b, n1 // t1),
        in_specs=[blk, mat, mat, pl.BlockSpec((t1, 2 * n2, c), lambda bi, i: (i, 0, 0))],
        out_specs=blk,
        compiler_params=_cparams(("parallel", "arbitrary")),
        name="hyena_mid",
    )(a5, g, h, kf)


def _kf_mid(a5, g):
    _, _, n1, n2, c2 = a5.shape
    c = c2 // 2
    t1 = _k1_tile(n1)
    return pl.pallas_call(
        _kf_mid_kernel,
        out_shape=jax.ShapeDtypeStruct((n1, 2 * n2, c), F32),
        grid=(n1 // t1,),
        in_specs=[pl.BlockSpec((1, 2, t1, n2, c2), lambda i: (0, 0, i, 0, 0)),
                  pl.BlockSpec((t1, 2 * n2, 2 * n2), lambda i: (i, 0, 0))],
        out_specs=pl.BlockSpec((t1, 2 * n2, c), lambda i: (i, 0, 0)),
        compiler_params=_cparams(("arbitrary",)),
        name="hyena_filter_spectrum",
    )(a5, g)


def _fnet_mid(a5, g, cm):
    b, _, n1, n2, c = a5.shape
    t1 = _k1_tile(n1)
    return pl.pallas_call(
        _fn_mid_kernel,
        out_shape=jax.ShapeDtypeStruct((b, n1, n2, c), MXU_DTYPE),
        grid=(b, n1 // t1),
        in_specs=[pl.BlockSpec((1, 2, t1, n2, c), lambda bi, i: (bi, 0, i, 0, 0)),
                  pl.BlockSpec((t1, 2 * n2, 2 * n2), lambda bi, i: (i, 0, 0)),
                  _const_spec((2, c, c))],
        out_specs=pl.BlockSpec((1, t1, n2, c), lambda bi, i: (bi, i, 0, 0)),
        compiler_params=_cparams(("parallel", "arbitrary")),
        name="fnet_mid",
    )(a5, g, cm)


def _fourier_mix(f):
    b, l, c = f.shape
    n1, n2 = _split(l, False)
    a = _left_matmul(_coarse_fwd_table(n1, n1), f.reshape(b, n1, n2 * c), MXU_DTYPE)
    g, _ = _fine_tables(n1, n2)
    ch = jnp.arange(c, dtype=jnp.int32)
    th = _angles(ch % FNET_GROUP, ch % FNET_GROUP, FNET_GROUP)
    same = (ch[:, None] // FNET_GROUP) == (ch[None, :] // FNET_GROUP)
    norm = 1.0 / math.sqrt(l * FNET_GROUP)
    cm = jnp.stack([jnp.where(same, jnp.cos(th), 0.0), jnp.where(same, jnp.sin(th), 0.0)]) * norm
    out = _fnet_mid(a.reshape(b, 2, n1, n2, c), g.astype(MXU_DTYPE), cm.astype(MXU_DTYPE))
    return jnp.swapaxes(out, 1, 2).reshape(b, l, c)


def _hy_filter_kernel(ze_ref, w1_ref, b1_ref, fr_ref, w2_ref, b2_ref, w3_ref, t_ref, dl_ref, o_ref):
    hp = lax.Precision.HIGHEST
    fr = fr_ref[...]
    h = jnp.sin(fr * (jnp.dot(ze_ref[...], w1_ref[...], precision=hp, preferred_element_type=F32) + b1_ref[...]))
    h = jnp.sin(fr * (jnp.dot(h, w2_ref[...], precision=hp, preferred_element_type=F32) + b2_ref[...]))
    h = jnp.dot(h, w3_ref[...], precision=hp, preferred_element_type=F32)
    c = dl_ref.shape[1]
    decay = jnp.exp(-t_ref[...] * dl_ref[...])
    hf = h[:, :c] * decay
    row = lax.broadcasted_iota(jnp.int32, (h.shape[0], 1), 0)
    hb = jnp.where(row == 0, 0.0, h[:, c:] * decay)
    nrm = jnp.sum(jnp.abs(hf), axis=0, keepdims=True) + jnp.sum(jnp.abs(hb), axis=0, keepdims=True)
    o_ref[:, :c] = hf / nrm
    o_ref[:, c:] = hb / nrm


def _hyena_filter_taps(l, hy):
    conv_w, conv_b, w1, b1, freq, w2, b2, w3, d_bias = hy
    emb = w1.shape[0]
    order = w1.shape[1]
    c = w3.shape[1] // 2
    t = jnp.linspace(0.0, 1.0, l, dtype=F32)[:, None]
    bands = (emb - 1) // 2
    fr = jnp.linspace(1e-4, bands - 1, bands, dtype=F32)
    ang = 2.0 * math.pi * jnp.arange(l, dtype=F32)[:, None] / l * fr
    z = jnp.concatenate([t, jnp.cos(ang), -jnp.sin(ang)], axis=-1)
    emb_pad = 128
    z = jnp.pad(z, ((0, 0), (0, emb_pad - emb)))
    w1p = jnp.pad(w1, ((0, emb_pad - emb), (0, 0)))
    min_decay = math.log(DECAY_TARGET_VAL) / SLOW_DECAY
    max_decay = math.log(DECAY_TARGET_VAL) / FAST_DECAY
    deltas = jnp.abs(jnp.linspace(min_decay, max_decay, c, dtype=F32))[None, :]
    args = (z, w1p, b1.reshape(1, order), freq.reshape(1, order), w2, b2.reshape(1, order), w3, t, deltas)
    return pl.pallas_call(
        _hy_filter_kernel,
        out_shape=jax.ShapeDtypeStruct((l, 2 * c), F32),
        grid=(1,),
        in_specs=[_const_spec(a.shape) for a in args],
        out_specs=_const_spec((l, 2 * c)),
        compiler_params=_cparams(("arbitrary",)),
        name="hyena_filter",
    )(*args)


def _hy_pre_kernel(u_ref, prev_ref, next_ref, w_ref, b_ref, x0_ref, z_ref):
    i = pl.program_id(1)
    tl = u_ref.shape[1]
    c = x0_ref.shape[2]
    u = u_ref[0]
    row = lax.broadcasted_iota(jnp.int32, (tl, 1), 0)
    before = jnp.where(i == 0, 0.0, prev_ref[0, 7:8, :])
    after = jnp.where(i == pl.num_programs(1) - 1, 0.0, next_ref[0, 0:1, :])
    um = jnp.where(row == 0, before, pltpu.roll(u, 1, 0))
    up = jnp.where(row == tl - 1, after, pltpu.roll(u, tl - 1, 0))
    out = b_ref[...] + um * w_ref[0:1, :] + u * w_ref[1:2, :] + up * w_ref[2:3, :]
    x0_ref[0] = out[:, :c]
    z_ref[0] = out[:, c:2 * c] * out[:, 2 * c:]


def _hyena_pre(u, conv_w, conv_b):
    b, l, w = u.shape
    c = w // 3
    tl = min(512, l)
    nb8 = tl // 8
    last8 = l // 8 - 1
    tok = lambda width: pl.BlockSpec((1, tl, width), lambda bi, i: (bi, i, 0))
    return pl.pallas_call(
        _hy_pre_kernel,
        out_shape=(jax.ShapeDtypeStruct((b, l, c), F32), jax.ShapeDtypeStruct((b, l, c), F32)),
        grid=(b, l // tl),
        in_specs=[tok(w),
                  pl.BlockSpec((1, 8, w), lambda bi, i: (bi, jnp.maximum(i * nb8 - 1, 0), 0)),
                  pl.BlockSpec((1, 8, w), lambda bi, i: (bi, jnp.minimum((i + 1) * nb8, last8), 0)),
                  _const_spec(conv_w.shape), _const_spec((1, w))],
        out_specs=(tok(c), tok(c)),
        compiler_params=_cparams(("parallel", "arbitrary")),
        name="hyena_pre",
    )(u, u, u, conv_w, conv_b.reshape(1, w))


def _hyena_spectrum(taps):
    l, c2 = taps.shape
    n = 2 * l
    n1, n2 = _split(n, True)
    a = _left_matmul(_coarse_fwd_table(n1, n1 // 2), taps.reshape(1, n1 // 2, n2 * c2), MXU_DTYPE)
    g, _ = _fine_tables(n1, n2)
    return _kf_mid(a.reshape(1, 2, n1, n2, c2), g.astype(MXU_DTYPE))


def _hyena_conv(z, kf):
    b, l, c = z.shape
    n = 2 * l
    n1, n2 = _split(n, True)
    a = _left_matmul(_coarse_fwd_table(n1, n1 // 2), z.reshape(b, n1 // 2, n2 * c), MXU_DTYPE)
    g, h = _fine_tables(n1, n2)
    bk = _hyena_mid(a.reshape(b, 2, n1, n2, c), g.astype(MXU_DTYPE), h.astype(MXU_DTYPE), kf)
    y = _left_matmul(_coarse_inv_table(n1, n1 // 2, n), bk.reshape(b, 2 * n1, n2 * c), F32)
    return y.reshape(b, l, c)


def _out_proj_kernel(x_ref, att_ref, f_ref, x0_ref, y_ref, z_ref, d_ref, gate_ref, w_ref, o_ref):
    a_w = att_ref.shape[2]
    f_w = f_ref.shape[2]
    hz = (x0_ref[0] * (y_ref[0] + d_ref[...] * z_ref[0])).astype(MXU_DTYPE)
    mix = (_dot(att_ref[0], w_ref[:a_w]) + _dot(f_ref[0], w_ref[a_w:a_w + f_w])
           + _dot(hz, w_ref[a_w + f_w:]))
    o_ref[0] = x_ref[0] + gate_ref[0] * mix


def _out_proj(x, att, f, x0, y, z, d_bias, gate, w_out):
    b, l, d = x.shape
    tm = min(512, l)
    tok = lambda w: pl.BlockSpec((1, tm, w), lambda bi, i: (bi, i, 0))
    c = x0.shape[2]
    return pl.pallas_call(
        _out_proj_kernel,
        out_shape=jax.ShapeDtypeStruct(x.shape, F32),
        grid=(b, l // tm),
        in_specs=[tok(d), tok(att.shape[2]), tok(f.shape[2]), tok(c), tok(c), tok(c),
                  _const_spec((1, c)), pl.BlockSpec((1, 1, d), lambda bi, i: (bi, 0, 0)),
                  _const_spec(w_out.shape)],
        out_specs=tok(d),
        compiler_params=_cparams(("parallel", "arbitrary")),
        name="out_proj",
    )(x, att, f, x0, y, z, d_bias.reshape(1, c), gate, w_out)


def _mlp_kernel(x_ref, sh_ref, sc_ref, gate_ref, g_ref, w1_ref, w2_ref, fg_ref, o_ref, *, chunk, final_norm):
    x = x_ref[0]
    h = (_rms(x, g_ref[...]) * (1.0 + sc_ref[0]) + sh_ref[0]).astype(MXU_DTYPE)
    acc = jnp.zeros(x.shape, F32)
    for c in range(w1_ref.shape[1] // chunk):
        a = jnp.maximum(_dot(h, w1_ref[:, c * chunk:(c + 1) * chunk]), 0.0)
        acc = acc + _dot((a * a).astype(MXU_DTYPE), w2_ref[c * chunk:(c + 1) * chunk, :])
    out = x + gate_ref[0] * acc
    if final_norm:
        out = _rms(out, fg_ref[...])
    o_ref[0] = out


def _mlp(x, shift, scale, gate, norm_g, w1, w2, final_g, final_norm):
    b, l, d = x.shape
    tm = min(512, l)
    tok = pl.BlockSpec((1, tm, d), lambda bi, i: (bi, i, 0))
    vec = pl.BlockSpec((1, 1, d), lambda bi, i: (bi, 0, 0))
    resident = lambda shape: pl.BlockSpec(shape, lambda bi, i: (0, 0), pipeline_mode=pl.Buffered(1))
    return pl.pallas_call(
        functools.partial(_mlp_kernel, chunk=512, final_norm=final_norm),
        out_shape=jax.ShapeDtypeStruct(x.shape, F32),
        grid=(b, l // tm),
        in_specs=[tok, vec, vec, vec, _const_spec((1, d)), resident(w1.shape), resident(w2.shape),
                  _const_spec((1, d))],
        out_specs=tok,
        compiler_params=_cparams(("parallel", "arbitrary")),
        name="mlp",
    )(x, shift, scale, gate, norm_g, w1, w2, final_g)


def _rope_partner_perm():
    half = AXIS_ROPE_W // 2
    partner, sign = [], []
    for j in range(QK_ROPE_W):
        first = (j % AXIS_ROPE_W) < half
        partner.append(j + half if first else j - half)
        sign.append(-1.0 if first else 1.0)
    return partner, sign


def _rope_slabs(n_tokens, rotary):
    partner, sign = _rope_partner_perm()
    if rotary:
        rows = n_tokens // ROPE_GRID_W
        row = jnp.repeat(jnp.arange(rows, dtype=F32), ROPE_GRID_W)
        col = jnp.tile(jnp.arange(ROPE_GRID_W, dtype=F32), rows)
        inv = ROPE_THETA ** (-jnp.arange(0, AXIS_ROPE_W, 2, dtype=F32) / AXIS_ROPE_W)
        ang = jnp.concatenate([row[:, None] * inv, col[:, None] * inv], axis=-1)
        cos16, sin16 = jnp.cos(ang), jnp.sin(ang)
        half = AXIS_ROPE_W // 2
        cols = jnp.array([(j // AXIS_ROPE_W) * half + j % half for j in range(QK_ROPE_W)])
        cos32 = cos16[:, cols]
        sin32 = sin16[:, cols] * jnp.array(sign, F32)
    else:
        cos32 = jnp.ones((n_tokens, QK_ROPE_W), F32)
        sin32 = jnp.zeros((n_tokens, QK_ROPE_W), F32)
    pad = HEAD_SLAB - QK_NOPE_W - QK_ROPE_W
    cos_t = jnp.concatenate([jnp.ones((n_tokens, QK_NOPE_W), F32), cos32, jnp.zeros((n_tokens, pad), F32)], axis=1)
    sin_t = jnp.concatenate([jnp.zeros((n_tokens, QK_NOPE_W), F32), sin32, jnp.zeros((n_tokens, pad), F32)], axis=1)
    return cos_t, sin_t


def _layer_weights(w_in, q_g, kv_g, w_uq, w_ukv):
    partner, _ = _rope_partner_perm()
    partner = jnp.array(partner)
    d = w_in.shape[0]
    pad = HEAD_SLAB - QK_NOPE_W - QK_ROPE_W
    zeros = lambda rows, w: jnp.zeros((rows, w), w_in.dtype)
    kr = w_in[:, OFF_K_ROPE:OFF_FNET]
    w_in_aug = jnp.concatenate([
        w_in[:, :OFF_K_ROPE],
        zeros(d, QK_NOPE_W), kr, zeros(d, pad),
        zeros(d, QK_NOPE_W), kr[:, partner], zeros(d, pad),
        w_in[:, OFF_FNET:]], axis=1)
    wq = w_uq.reshape(Q_LORA_W, HEADS, QK_NOPE_W + QK_ROPE_W)
    qz = jnp.zeros((Q_LORA_W, HEADS, pad), w_uq.dtype)
    wq_a = jnp.concatenate([wq, qz], axis=2).reshape(Q_LORA_W, HEADS * HEAD_SLAB)
    wq_b = jnp.concatenate([jnp.zeros((Q_LORA_W, HEADS, QK_NOPE_W), w_uq.dtype),
                            wq[:, :, QK_NOPE_W:][:, :, partner], qz], axis=2).reshape(Q_LORA_W, HEADS * HEAD_SLAB)
    wkv = w_ukv.reshape(KV_LORA_W, HEADS, QK_NOPE_W + V_HEAD_W)
    wk = jnp.concatenate([wkv[:, :, :QK_NOPE_W], jnp.zeros((KV_LORA_W, HEADS, HEAD_SLAB - QK_NOPE_W), w_ukv.dtype)],
                         axis=2).reshape(KV_LORA_W, HEADS * HEAD_SLAB)
    wv = wkv[:, :, QK_NOPE_W:].reshape(KV_LORA_W, HEADS * V_HEAD_W)
    cast = lambda a: a.astype(MXU_DTYPE)
    return {"w_in": cast(w_in_aug), "q_g": q_g.reshape(1, -1), "kv_g": kv_g.reshape(1, -1),
            "wq_a": cast(wq_a), "wq_b": cast(wq_b), "wk": cast(wk), "wv": cast(wv)}


def _heads_and_residual(x, att, f, u, hy, gate, w_out):
    conv_w, conv_b, d_bias = hy[0], hy[1], hy[8]
    fm = _fourier_mix(f)
    x0, z = _hyena_pre(u, conv_w, conv_b)
    kf = _hyena_spectrum(_hyena_filter_taps(x.shape[1], hy))
    y = _hyena_conv(z, kf)
    return _out_proj(x, att, fm, x0, y, z, d_bias, gate, w_out)


def kernel(x, c, ctx, c_ctx, norm1_g, norm2_g, w_mod, b_mod, w_in, q_norm_g, kv_norm_g, w_uq, w_ukv,
           hy_conv_w, hy_conv_b, hy_w1, hy_b1, hy_freq, hy_w2, hy_b2, hy_w3, hy_d, w_out, w_mlp1,
           w_mlp2, final_norm_g):
    b, l, d = x.shape
    lc = ctx.shape[1]
    depth = w_mod.shape[0]
    rows = 16
    cc = jnp.concatenate([c, c_ctx[None, :], jnp.zeros((rows - b - 1, d), F32)], axis=0)
    mod = _modulation(cc, w_mod, b_mod)
    rope_x = _rope_slabs(l, True)
    rope_c = _rope_slabs(lc, False)
    final_g = final_norm_g.reshape(1, d)
    xc = ctx
    for li in range(depth):
        last = li == depth - 1
        mx = mod[li, :b].reshape(b, 1, N_MODULATION, d)
        mc = jnp.broadcast_to(mod[li, b].reshape(1, 1, N_MODULATION, d), (b, 1, N_MODULATION, d))
        sh1, sc1, g1, sh2, sc2, g2 = [mx[:, :, i] for i in range(N_MODULATION)]
        csh1, csc1, cg1, csh2, csc2, cg2 = [mc[:, :, i] for i in range(N_MODULATION)]
        hy = (hy_conv_w[li], hy_conv_b[li], hy_w1[li], hy_b1[li], hy_freq[li], hy_w2[li], hy_b2[li],
              hy_w3[li], hy_d[li])
        lw = _layer_weights(w_in[li], q_norm_g[li], kv_norm_g[li], w_uq[li], w_ukv[li])
        n1g = norm1_g[li].reshape(1, d)
        n2g = norm2_g[li].reshape(1, d)
        wo = w_out[li].astype(MXU_DTYPE)
        w1 = w_mlp1[li].astype(MXU_DTYPE)
        w2 = w_mlp2[li].astype(MXU_DTYPE)

        q_x, k_x, v_x, f_x, u_x = _in_proj(x, sh1, sc1, n1g, lw, *rope_x)
        q_c, k_c, v_c, f_c, u_c = _in_proj(xc, csh1, csc1, n1g, lw, *rope_c)
        att_x = _attention(q_x, [(k_x, v_x), (k_c, v_c)])
        x = _heads_and_residual(x, att_x, f_x, u_x, hy, g1, wo)
        x = _mlp(x, sh2, sc2, g2, n2g, w1, w2, final_g, last)
        if not last:
            att_c = _attention(q_c, [(k_c, v_c)])
            xc = _heads_and_residual(xc, att_c, f_c, u_c, hy, cg1, wo)
            xc = _mlp(xc, csh2, csc2, cg2, n2g, w1, w2, final_g, False)
    return x
```

```python
import functools
import math

import jax
import jax.numpy as jnp
from jax import lax
from jax.experimental import pallas as pl
from jax.experimental.pallas import tpu as pltpu

F32 = jnp.float32
MXU_DTYPE = jnp.bfloat16

HEADS = 8
QK_NOPE_W = 64
QK_ROPE_W = 32
V_HEAD_W = 64
Q_LORA_W = 384
KV_LORA_W = 256
AXIS_ROPE_W = QK_ROPE_W // 2
ROPE_THETA = 10000.0
ROPE_GRID_W = 64
FNET_GROUP = 64
FNET_WIDTH = 256
HYENA_WIDTH = 256
N_MODULATION = 6
NORM_EPS = 1e-6
DECAY_TARGET_VAL = 1e-2
FAST_DECAY = 0.3
SLOW_DECAY = 1.5

HEAD_SLAB = 128
V_ONES_ROWS = 16
V_SLAB = V_HEAD_W + V_ONES_ROWS
OFF_KV_LORA = Q_LORA_W
OFF_K_ROPE = OFF_KV_LORA + KV_LORA_W
OFF_FNET = OFF_K_ROPE + QK_ROPE_W
OFF_HYENA = OFF_FNET + FNET_WIDTH

AUG_KR = OFF_K_ROPE
AUG_KR_SWAP = AUG_KR + HEAD_SLAB
AUG_F = AUG_KR_SWAP + HEAD_SLAB
AUG_H = AUG_F + FNET_WIDTH
AUG_W = AUG_H + 3 * HYENA_WIDTH

V7X_VMEM_LIMIT = 56 * 1024 * 1024


def _cparams(sem):
    return pltpu.CompilerParams(dimension_semantics=sem, vmem_limit_bytes=V7X_VMEM_LIMIT)


def _dot(a, b):
    return jnp.dot(a, b, preferred_element_type=F32)


def _rms(x, g):
    return x * lax.rsqrt(jnp.mean(x * x, axis=-1, keepdims=True) + NORM_EPS) * g


def _const_spec(shape):
    n = len(shape)
    return pl.BlockSpec(shape, lambda *_: (0,) * n)


def _mod_kernel(c_ref, w_ref, b_ref, o_ref):
    c = c_ref[...]
    s = c / (1.0 + jnp.exp(-c))
    o_ref[0] = _dot(s.astype(MXU_DTYPE), w_ref[0].astype(MXU_DTYPE)) + b_ref[0]


def _modulation(cc, w_mod, b_mod):
    depth, d, n = w_mod.shape
    r = cc.shape[0]
    tn = 1024
    return pl.pallas_call(
        _mod_kernel,
        out_shape=jax.ShapeDtypeStruct((depth, r, n), F32),
        grid=(depth, n // tn),
        in_specs=[pl.BlockSpec((r, d), lambda l, j: (0, 0)),
                  pl.BlockSpec((1, d, tn), lambda l, j: (l, 0, j)),
                  pl.BlockSpec((1, 1, tn), lambda l, j: (l, 0, j))],
        out_specs=pl.BlockSpec((1, r, tn), lambda l, j: (l, 0, j)),
        compiler_params=_cparams(("arbitrary", "arbitrary")),
        name="modulation",
    )(cc, w_mod, b_mod.reshape(depth, 1, n))


def _in_proj_kernel(x_ref, sh_ref, sc_ref, g_ref, win_ref, qg_ref, kvg_ref, wqa_ref, wqb_ref,
                    wk_ref, wv_ref, cos_ref, sin_ref, cos_t_ref, sin_t_ref,
                    qt_ref, k_ref, vt_ref, f_ref, u_ref, *, q_scale):
    h = _rms(x_ref[0], g_ref[...]) * (1.0 + sc_ref[0]) + sh_ref[0]
    p = _dot(h.astype(MXU_DTYPE), win_ref[...])
    cq_t = _rms(p[:, :Q_LORA_W], qg_ref[...]).T.astype(MXU_DTYPE)
    qa_t = _dot(wqa_ref[...], cq_t)
    qb_t = _dot(wqb_ref[...], cq_t)
    cos_t = cos_t_ref[...]
    sin_t = sin_t_ref[...]
    ckv = _rms(p[:, OFF_KV_LORA:OFF_K_ROPE], kvg_ref[...])
    kn = _dot(ckv.astype(MXU_DTYPE), wk_ref[...])
    kr = p[:, AUG_KR:AUG_KR_SWAP] * cos_ref[...] + p[:, AUG_KR_SWAP:AUG_F] * sin_ref[...]
    for hd in range(HEADS):
        hs = slice(hd * HEAD_SLAB, (hd + 1) * HEAD_SLAB)
        qt_ref[0, hs, :] = ((qa_t[hs] * cos_t + qb_t[hs] * sin_t) * q_scale).astype(qt_ref.dtype)
        k_ref[0, :, hs] = (kn[:, hs] + kr).astype(k_ref.dtype)
    vt = _dot(wv_ref[...], ckv.T.astype(MXU_DTYPE)).astype(vt_ref.dtype)
    ones = jnp.ones((V_ONES_ROWS, vt.shape[1]), vt_ref.dtype)
    for hd in range(HEADS):
        vt_ref[0, hd * V_SLAB:hd * V_SLAB + V_HEAD_W, :] = vt[hd * V_HEAD_W:(hd + 1) * V_HEAD_W]
        vt_ref[0, hd * V_SLAB + V_HEAD_W:(hd + 1) * V_SLAB, :] = ones
    f_ref[0] = p[:, AUG_F:AUG_H].astype(f_ref.dtype)
    u_ref[0] = p[:, AUG_H:AUG_W]


def _in_proj(x, shift, scale, norm_g, lw, rope):
    b, l, d = x.shape
    tm = min(512, l)
    q_scale = math.log2(math.e) / math.sqrt(QK_NOPE_W + QK_ROPE_W)
    tok = lambda w: pl.BlockSpec((1, tm, w), lambda bi, i: (bi, i, 0))
    tok_t = lambda w: pl.BlockSpec((1, w, tm), lambda bi, i: (bi, 0, i))
    vec = pl.BlockSpec((1, 1, d), lambda bi, i: (bi, 0, 0))
    tab = pl.BlockSpec((tm, HEAD_SLAB), lambda bi, i: (i, 0))
    tab_t = pl.BlockSpec((HEAD_SLAB, tm), lambda bi, i: (0, i))
    qk_w = HEADS * HEAD_SLAB
    v_w = HEADS * V_HEAD_W
    return pl.pallas_call(
        functools.partial(_in_proj_kernel, q_scale=q_scale),
        out_shape=(jax.ShapeDtypeStruct((b, qk_w, l), MXU_DTYPE),
                   jax.ShapeDtypeStruct((b, l, qk_w), MXU_DTYPE),
                   jax.ShapeDtypeStruct((b, HEADS * V_SLAB, l), MXU_DTYPE),
                   jax.ShapeDtypeStruct((b, l, FNET_WIDTH), MXU_DTYPE),
                   jax.ShapeDtypeStruct((b, l, 3 * HYENA_WIDTH), F32)),
        grid=(b, l // tm),
        in_specs=[tok(d), vec, vec, _const_spec((1, d)), _const_spec((d, AUG_W)),
                  _const_spec((1, Q_LORA_W)), _const_spec((1, KV_LORA_W)),
                  _const_spec((qk_w, Q_LORA_W)), _const_spec((qk_w, Q_LORA_W)),
                  _const_spec((KV_LORA_W, qk_w)), _const_spec((v_w, KV_LORA_W)), tab, tab, tab_t, tab_t],
        out_specs=(tok_t(qk_w), tok(qk_w), tok_t(HEADS * V_SLAB), tok(FNET_WIDTH), tok(3 * HYENA_WIDTH)),
        compiler_params=_cparams(("parallel", "arbitrary")),
        name="in_proj",
    )(x, shift, scale, norm_g, lw["w_in"], lw["q_g"], lw["kv_g"], lw["wq_a_t"], lw["wq_b_t"],
      lw["wk"], lw["wv_t"], *rope)


def _attn_kernel(*refs, nseg, tk):
    qt_ref = refs[0]
    segs = [(refs[1 + 2 * s], refs[2 + 2 * s]) for s in range(nseg)]
    o_ref, s_ref, p_ref = refs[1 + 2 * nseg:]
    tq = qt_ref.shape[2]
    s_total = s_ref.shape[1]

    def scores(hd):
        qt = qt_ref[0, hd * HEAD_SLAB:(hd + 1) * HEAD_SLAB, :]
        mx = jnp.full((1, tq), -jnp.inf, F32)
        off = 0
        for k_ref, _ in segs:
            s_len = k_ref.shape[1]
            for lo in range(0, s_len, tk):
                hi = min(lo + tk, s_len)
                s = _dot(k_ref[0, lo:hi, hd * HEAD_SLAB:(hd + 1) * HEAD_SLAB], qt)
                s_ref[hd, off + lo:off + hi, :] = s
                mx = jnp.maximum(mx, jnp.max(s, axis=0, keepdims=True))
            off += s_len
        return mx

    def probs(hd, mx):
        for lo in range(0, s_total, tk):
            hi = min(lo + tk, s_total)
            p_ref[hd, lo:hi, :] = jnp.exp2(s_ref[hd, lo:hi, :] - mx).astype(p_ref.dtype)

    def values(hd):
        acc = jnp.zeros((V_SLAB, tq), F32)
        off = 0
        for _, vt_ref in segs:
            s_len = vt_ref.shape[2]
            acc = acc + _dot(vt_ref[0, hd * V_SLAB:(hd + 1) * V_SLAB, :], p_ref[hd, off:off + s_len, :])
            off += s_len
        return acc[:V_HEAD_W] / acc[V_HEAD_W:V_HEAD_W + 1]

    mx0 = scores(0)
    probs(0, mx0)
    mx1 = scores(1)
    o0 = values(0)
    probs(1, mx1)
    o1 = values(1)
    o_ref[0] = jnp.concatenate([o0, o1], axis=0).T.astype(o_ref.dtype)


def _attention(qt, kv_segs):
    b, _, l = qt.shape
    tq = min(512, l)
    pair_qk = 2 * HEAD_SLAB
    pair_v = 2 * V_HEAD_W
    in_specs = [pl.BlockSpec((1, pair_qk, tq), lambda bi, hp, i: (bi, hp, i))]
    args = [qt]
    for k, vt in kv_segs:
        s_len = k.shape[1]
        in_specs.append(pl.BlockSpec((1, s_len, pair_qk), lambda bi, hp, i: (bi, 0, hp)))
        in_specs.append(pl.BlockSpec((1, 2 * V_SLAB, s_len), lambda bi, hp, i: (bi, hp, 0)))
        args += [k, vt]
    s_total = sum(k.shape[1] for k, _ in kv_segs)
    return pl.pallas_call(
        functools.partial(_attn_kernel, nseg=len(kv_segs), tk=256),
        out_shape=jax.ShapeDtypeStruct((b, l, HEADS * V_HEAD_W), MXU_DTYPE),
        grid=(b, HEADS // 2, l // tq),
        in_specs=in_specs,
        out_specs=pl.BlockSpec((1, tq, pair_v), lambda bi, hp, i: (bi, i, hp)),
        scratch_shapes=[pltpu.VMEM((2, s_total, tq), F32), pltpu.VMEM((2, s_total, tq), MXU_DTYPE)],
        compiler_params=_cparams(("parallel", "parallel", "arbitrary")),
        name="attention",
    )(*args)


def _split(n, zero_padded):
    n1 = 1 << (int(math.log2(n)) // 2)
    n2 = n // n1
    if zero_padded and n1 // 2 < 16 and n2 > n1:
        n1, n2 = n2, n1
    return n1, n2


def _angles(rows, cols, n):
    idx = (rows[:, None] * cols[None, :]) % n
    return idx.astype(F32) * (2.0 * math.pi / n)


def _coarse_fwd_table(n1, k_in):
    th = _angles(jnp.arange(n1, dtype=jnp.int32), jnp.arange(k_in, dtype=jnp.int32), n1)
    return jnp.concatenate([jnp.cos(th), -jnp.sin(th)], axis=0)


def _coarse_inv_table(n1, t_out, n):
    th = _angles(jnp.arange(t_out, dtype=jnp.int32), jnp.arange(n1, dtype=jnp.int32), n1)
    return jnp.concatenate([jnp.cos(th), -jnp.sin(th)], axis=1) * (1.0 / n)


def _fine_tables(n1, n2):
    n = n1 * n2
    k = (jnp.arange(n1, dtype=jnp.int32)[:, None] + n1 * jnp.arange(n2, dtype=jnp.int32)[None, :])
    idx = (k[:, :, None] * jnp.arange(n2, dtype=jnp.int32)[None, None, :]) % n
    ph = idx.astype(F32) * (2.0 * math.pi / n)
    c, s = jnp.cos(ph), jnp.sin(ph)
    fwd = jnp.concatenate([jnp.concatenate([c, s], axis=2), jnp.concatenate([-s, c], axis=2)], axis=1)
    return fwd, jnp.swapaxes(fwd, 1, 2)


def _lmm_kernel(t_ref, x_ref, o_ref):
    o_ref[0] = _dot(t_ref[...], x_ref[0].astype(MXU_DTYPE)).astype(o_ref.dtype)


def _left_matmul(table, x, out_dtype):
    b, k, n = x.shape
    m = table.shape[0]
    tc = min(2048, n)
    return pl.pallas_call(
        _lmm_kernel,
        out_shape=jax.ShapeDtypeStruct((b, m, n), out_dtype),
        grid=(b, n // tc),
        in_specs=[_const_spec((m, k)), pl.BlockSpec((1, k, tc), lambda bi, j: (bi, 0, j))],
        out_specs=pl.BlockSpec((1, m, tc), lambda bi, j: (bi, 0, j)),
        compiler_params=_cparams(("parallel", "arbitrary")),
        name="dft_coarse",
    )(table.astype(MXU_DTYPE), x)


def _fine_fwd(x_ref, g_ref, j):
    n2 = x_ref.shape[3]
    x = x_ref[0, :, j].reshape(2 * n2, x_ref.shape[4])
    return _dot(g_ref[j], x)


def _hy_mid_kernel(x_ref, g_ref, h_ref, kf_ref, o_ref):
    n2 = x_ref.shape[3]
    for j in range(x_ref.shape[2]):
        zf = _fine_fwd(x_ref, g_ref, j)
        zr, zi = zf[:n2], zf[n2:]
        kr, ki = kf_ref[j, :n2], kf_ref[j, n2:]
        y = jnp.concatenate([zr * kr - zi * ki, zr * ki + zi * kr], axis=0).astype(MXU_DTYPE)
        bk = _dot(h_ref[j], y)
        o_ref[0, 0, j] = bk[:n2].astype(o_ref.dtype)
        o_ref[0, 1, j] = bk[n2:].astype(o_ref.dtype)


def _kf_mid_kernel(x_ref, g_ref, o_ref):
    n2 = x_ref.shape[3]
    c = o_ref.shape[2]
    for j in range(x_ref.shape[2]):
        zf = _fine_fwd(x_ref, g_ref, j)
        o_ref[j, :n2] = zf[:n2, :c] + zf[:n2, c:]
        o_ref[j, n2:] = zf[n2:, :c] - zf[n2:, c:]


def _fn_mid_kernel(x_ref, g_ref, cm_ref, o_ref):
    for j in range(x_ref.shape[2]):
        pf = _fine_fwd(x_ref, g_ref, j).astype(MXU_DTYPE)
        n2 = x_ref.shape[3]
        o_ref[0, j] = (_dot(pf[:n2], cm_ref[0]) + _dot(pf[n2:], cm_ref[1])).astype(o_ref.dtype)


def _k1_tile(n1):
    return min(8, n1)


def _hyena_mid(a5, g, h, kf):
    b, _, n1, n2, c = a5.shape
    t1 = _k1_tile(n1)
    blk = pl.BlockSpec((1, 2, t1, n2, c), lambda bi, i: (bi, 0, i, 0, 0))
    mat = pl.BlockSpec((t1, 2 * n2, 2 * n2), lambda bi, i: (i, 0, 0))
    return pl.pallas_call(
        _hy_mid_kernel,
        out_shape=jax.ShapeDtypeStruct(a5.shape, MXU_DTYPE),
        grid=(b, n1 // t1),
        in_specs=[blk, mat, mat, pl.BlockSpec((t1, 2 * n2, c), lambda bi, i: (i, 0, 0))],
        out_specs=blk,
        compiler_params=_cparams(("parallel", "arbitrary")),
        name="hyena_mid",
    )(a5, g, h, kf)


def _kf_mid(a5, g):
    _, _, n1, n2, c2 = a5.shape
    c = c2 // 2
    t1 = _k1_tile(n1)
    return pl.pallas_call(
        _kf_mid_kernel,
        out_shape=jax.ShapeDtypeStruct((n1, 2 * n2, c), F32),
        grid=(n1 // t1,),
        in_specs=[pl.BlockSpec((1, 2, t1, n2, c2), lambda i: (0, 0, i, 0, 0)),
                  pl.BlockSpec((t1, 2 * n2, 2 * n2), lambda i: (i, 0, 0))],
        out_specs=pl.BlockSpec((t1, 2 * n2, c), lambda i: (i, 0, 0)),
        compiler_params=_cparams(("arbitrary",)),
        name="hyena_filter_spectrum",
    )(a5, g)


def _fnet_mid(a5, g, cm):
    b, _, n1, n2, c = a5.shape
    t1 = _k1_tile(n1)
    return pl.pallas_call(
        _fn_mid_kernel,
        out_shape=jax.ShapeDtypeStruct((b, n1, n2, c), MXU_DTYPE),
        grid=(b, n1 // t1),
        in_specs=[pl.BlockSpec((1, 2, t1, n2, c), lambda bi, i: (bi, 0, i, 0, 0)),
                  pl.BlockSpec((t1, 2 * n2, 2 * n2), lambda bi, i: (i, 0, 0)),
                  _const_spec((2, c, c))],
        out_specs=pl.BlockSpec((1, t1, n2, c), lambda bi, i: (bi, i, 0, 0)),
        compiler_params=_cparams(("parallel", "arbitrary")),
        name="fnet_mid",
    )(a5, g, cm)


def _fourier_mix(f):
    b, l, c = f.shape
    n1, n2 = _split(l, False)
    a = _left_matmul(_coarse_fwd_table(n1, n1), f.reshape(b, n1, n2 * c), MXU_DTYPE)
    g, _ = _fine_tables(n1, n2)
    ch = jnp.arange(c, dtype=jnp.int32)
    th = _angles(ch % FNET_GROUP, ch % FNET_GROUP, FNET_GROUP)
    same = (ch[:, None] // FNET_GROUP) == (ch[None, :] // FNET_GROUP)
    norm = 1.0 / math.sqrt(l * FNET_GROUP)
    cm = jnp.stack([jnp.where(same, jnp.cos(th), 0.0), jnp.where(same, jnp.sin(th), 0.0)]) * norm
    out = _fnet_mid(a.reshape(b, 2, n1, n2, c), g.astype(MXU_DTYPE), cm.astype(MXU_DTYPE))
    return jnp.swapaxes(out, 1, 2).reshape(b, l, c)


def _hy_filter_kernel(ze_ref, w1_ref, b1_ref, fr_ref, w2_ref, b2_ref, w3_ref, t_ref, dl_ref, o_ref):
    hp = lax.Precision.HIGHEST
    fr = fr_ref[...]
    h = jnp.sin(fr * (jnp.dot(ze_ref[...], w1_ref[...], precision=hp, preferred_element_type=F32) + b1_ref[...]))
    h = jnp.sin(fr * (jnp.dot(h, w2_ref[...], precision=hp, preferred_element_type=F32) + b2_ref[...]))
    h = jnp.dot(h, w3_ref[...], precision=hp, preferred_element_type=F32)
    c = dl_ref.shape[1]
    decay = jnp.exp(-t_ref[...] * dl_ref[...])
    hf = h[:, :c] * decay
    row = lax.broadcasted_iota(jnp.int32, (h.shape[0], 1), 0)
    hb = jnp.where(row == 0, 0.0, h[:, c:] * decay)
    nrm = jnp.sum(jnp.abs(hf), axis=0, keepdims=True) + jnp.sum(jnp.abs(hb), axis=0, keepdims=True)
    o_ref[:, :c] = hf / nrm
    o_ref[:, c:] = hb / nrm


def _hyena_filter_taps(l, hy):
    conv_w, conv_b, w1, b1, freq, w2, b2, w3, d_bias = hy
    emb = w1.shape[0]
    order = w1.shape[1]
    c = w3.shape[1] // 2
    t = jnp.linspace(0.0, 1.0, l, dtype=F32)[:, None]
    bands = (emb - 1) // 2
    fr = jnp.linspace(1e-4, bands - 1, bands, dtype=F32)
    ang = 2.0 * math.pi * jnp.arange(l, dtype=F32)[:, None] / l * fr
    z = jnp.concatenate([t, jnp.cos(ang), -jnp.sin(ang)], axis=-1)
    emb_pad = 128
    z = jnp.pad(z, ((0, 0), (0, emb_pad - emb)))
    w1p = jnp.pad(w1, ((0, emb_pad - emb), (0, 0)))
    min_decay = math.log(DECAY_TARGET_VAL) / SLOW_DECAY
    max_decay = math.log(DECAY_TARGET_VAL) / FAST_DECAY
    deltas = jnp.abs(jnp.linspace(min_decay, max_decay, c, dtype=F32))[None, :]
    args = (z, w1p, b1.reshape(1, order), freq.reshape(1, order), w2, b2.reshape(1, order), w3, t, deltas)
    return pl.pallas_call(
        _hy_filter_kernel,
        out_shape=jax.ShapeDtypeStruct((l, 2 * c), F32),
        grid=(1,),
        in_specs=[_const_spec(a.shape) for a in args],
        out_specs=_const_spec((l, 2 * c)),
        compiler_params=_cparams(("arbitrary",)),
        name="hyena_filter",
    )(*args)


def _hy_pre_kernel(u_ref, prev_ref, next_ref, w_ref, b_ref, x0_ref, z_ref):
    i = pl.program_id(1)
    tl = u_ref.shape[1]
    c = x0_ref.shape[2]
    u = u_ref[0]
    row = lax.broadcasted_iota(jnp.int32, (tl, 1), 0)
    before = jnp.where(i == 0, 0.0, prev_ref[0, 7:8, :])
    after = jnp.where(i == pl.num_programs(1) - 1, 0.0, next_ref[0, 0:1, :])
    um = jnp.where(row == 0, before, pltpu.roll(u, 1, 0))
    up = jnp.where(row == tl - 1, after, pltpu.roll(u, tl - 1, 0))
    out = b_ref[...] + um * w_ref[0:1, :] + u * w_ref[1:2, :] + up * w_ref[2:3, :]
    x0_ref[0] = out[:, :c]
    z_ref[0] = out[:, c:2 * c] * out[:, 2 * c:]


def _hyena_pre(u, conv_w, conv_b):
    b, l, w = u.shape
    c = w // 3
    tl = min(512, l)
    nb8 = tl // 8
    last8 = l // 8 - 1
    tok = lambda width: pl.BlockSpec((1, tl, width), lambda bi, i: (bi, i, 0))
    return pl.pallas_call(
        _hy_pre_kernel,
        out_shape=(jax.ShapeDtypeStruct((b, l, c), F32), jax.ShapeDtypeStruct((b, l, c), F32)),
        grid=(b, l // tl),
        in_specs=[tok(w),
                  pl.BlockSpec((1, 8, w), lambda bi, i: (bi, jnp.maximum(i * nb8 - 1, 0), 0)),
                  pl.BlockSpec((1, 8, w), lambda bi, i: (bi, jnp.minimum((i + 1) * nb8, last8), 0)),
                  _const_spec(conv_w.shape), _const_spec((1, w))],
        out_specs=(tok(c), tok(c)),
        compiler_params=_cparams(("parallel", "arbitrary")),
        name="hyena_pre",
    )(u, u, u, conv_w, conv_b.reshape(1, w))


def _hyena_spectrum(taps):
    l, c2 = taps.shape
    n = 2 * l
    n1, n2 = _split(n, True)
    a = _left_matmul(_coarse_fwd_table(n1, n1 // 2), taps.reshape(1, n1 // 2, n2 * c2), MXU_DTYPE)
    g, _ = _fine_tables(n1, n2)
    return _kf_mid(a.reshape(1, 2, n1, n2, c2), g.astype(MXU_DTYPE))


def _hyena_conv(z, kf):
    b, l, c = z.shape
    n = 2 * l
    n1, n2 = _split(n, True)
    a = _left_matmul(_coarse_fwd_table(n1, n1 // 2), z.reshape(b, n1 // 2, n2 * c), MXU_DTYPE)
    g, h = _fine_tables(n1, n2)
    bk = _hyena_mid(a.reshape(b, 2, n1, n2, c), g.astype(MXU_DTYPE), h.astype(MXU_DTYPE), kf)
    y = _left_matmul(_coarse_inv_table(n1, n1 // 2, n), bk.reshape(b, 2 * n1, n2 * c), F32)
    return y.reshape(b, l, c)


def _out_proj_kernel(x_ref, att_ref, f_ref, x0_ref, y_ref, z_ref, d_ref, gate_ref, w_ref, o_ref):
    a_w = att_ref.shape[2]
    f_w = f_ref.shape[2]
    hz = (x0_ref[0] * (y_ref[0] + d_ref[...] * z_ref[0])).astype(MXU_DTYPE)
    mix = (_dot(att_ref[0], w_ref[:a_w]) + _dot(f_ref[0], w_ref[a_w:a_w + f_w])
           + _dot(hz, w_ref[a_w + f_w:]))
    o_ref[0] = x_ref[0] + gate_ref[0] * mix


def _out_proj(x, att, f, x0, y, z, d_bias, gate, w_out):
    b, l, d = x.shape
    tm = min(512, l)
    tok = lambda w: pl.BlockSpec((1, tm, w), lambda bi, i: (bi, i, 0))
    c = x0.shape[2]
    return pl.pallas_call(
        _out_proj_kernel,
        out_shape=jax.ShapeDtypeStruct(x.shape, F32),
        grid=(b, l // tm),
        in_specs=[tok(d), tok(att.shape[2]), tok(f.shape[2]), tok(c), tok(c), tok(c),
                  _const_spec((1, c)), pl.BlockSpec((1, 1, d), lambda bi, i: (bi, 0, 0)),
                  _const_spec(w_out.shape)],
        out_specs=tok(d),
        compiler_params=_cparams(("parallel", "arbitrary")),
        name="out_proj",
    )(x, att, f, x0, y, z, d_bias.reshape(1, c), gate, w_out)


def _mlp_kernel(x_ref, sh_ref, sc_ref, gate_ref, g_ref, w1_ref, w2_ref, fg_ref, o_ref, *, chunk, final_norm):
    x = x_ref[0]
    h = (_rms(x, g_ref[...]) * (1.0 + sc_ref[0]) + sh_ref[0]).astype(MXU_DTYPE)
    acc = jnp.zeros(x.shape, F32)
    for c in range(w1_ref.shape[1] // chunk):
        a = jnp.maximum(_dot(h, w1_ref[:, c * chunk:(c + 1) * chunk]), 0.0)
        acc = acc + _dot((a * a).astype(MXU_DTYPE), w2_ref[c * chunk:(c + 1) * chunk, :])
    out = x + gate_ref[0] * acc
    if final_norm:
        out = _rms(out, fg_ref[...])
    o_ref[0] = out


def _mlp(x, shift, scale, gate, norm_g, w1, w2, final_g, final_norm):
    b, l, d = x.shape
    tm = min(512, l)
    tok = pl.BlockSpec((1, tm, d), lambda bi, i: (bi, i, 0))
    vec = pl.BlockSpec((1, 1, d), lambda bi, i: (bi, 0, 0))
    resident = lambda shape: pl.BlockSpec(shape, lambda bi, i: (0, 0), pipeline_mode=pl.Buffered(1))
    return pl.pallas_call(
        functools.partial(_mlp_kernel, chunk=512, final_norm=final_norm),
        out_shape=jax.ShapeDtypeStruct(x.shape, F32),
        grid=(b, l // tm),
        in_specs=[tok, vec, vec, vec, _const_spec((1, d)), resident(w1.shape), resident(w2.shape),
                  _const_spec((1, d))],
        out_specs=tok,
        compiler_params=_cparams(("parallel", "arbitrary")),
        name="mlp",
    )(x, shift, scale, gate, norm_g, w1, w2, final_g)


def _rope_partner_perm():
    half = AXIS_ROPE_W // 2
    partner, sign = [], []
    for j in range(QK_ROPE_W):
        first = (j % AXIS_ROPE_W) < half
        partner.append(j + half if first else j - half)
        sign.append(-1.0 if first else 1.0)
    return partner, sign


def _rope_slabs(n_tokens, rotary):
    partner, sign = _rope_partner_perm()
    if rotary:
        rows = n_tokens // ROPE_GRID_W
        row = jnp.repeat(jnp.arange(rows, dtype=F32), ROPE_GRID_W)
        col = jnp.tile(jnp.arange(ROPE_GRID_W, dtype=F32), rows)
        inv = ROPE_THETA ** (-jnp.arange(0, AXIS_ROPE_W, 2, dtype=F32) / AXIS_ROPE_W)
        ang = jnp.concatenate([row[:, None] * inv, col[:, None] * inv], axis=-1)
        cos16, sin16 = jnp.cos(ang), jnp.sin(ang)
        half = AXIS_ROPE_W // 2
        cols = jnp.array([(j // AXIS_ROPE_W) * half + j % half for j in range(QK_ROPE_W)])
        cos32 = cos16[:, cols]
        sin32 = sin16[:, cols] * jnp.array(sign, F32)
    else:
        cos32 = jnp.ones((n_tokens, QK_ROPE_W), F32)
        sin32 = jnp.zeros((n_tokens, QK_ROPE_W), F32)
    pad = HEAD_SLAB - QK_NOPE_W - QK_ROPE_W
    cos_t = jnp.concatenate([jnp.ones((n_tokens, QK_NOPE_W), F32), cos32, jnp.zeros((n_tokens, pad), F32)], axis=1)
    sin_t = jnp.concatenate([jnp.zeros((n_tokens, QK_NOPE_W), F32), sin32, jnp.zeros((n_tokens, pad), F32)], axis=1)
    return cos_t, sin_t, cos_t.T, sin_t.T


def _layer_weights(w_in, q_g, kv_g, w_uq, w_ukv):
    partner, _ = _rope_partner_perm()
    partner = jnp.array(partner)
    d = w_in.shape[0]
    pad = HEAD_SLAB - QK_NOPE_W - QK_ROPE_W
    zeros = lambda rows, w: jnp.zeros((rows, w), w_in.dtype)
    kr = w_in[:, OFF_K_ROPE:OFF_FNET]
    w_in_aug = jnp.concatenate([
        w_in[:, :OFF_K_ROPE],
        zeros(d, QK_NOPE_W), kr, zeros(d, pad),
        zeros(d, QK_NOPE_W), kr[:, partner], zeros(d, pad),
        w_in[:, OFF_FNET:]], axis=1)
    wq = w_uq.reshape(Q_LORA_W, HEADS, QK_NOPE_W + QK_ROPE_W)
    qz = jnp.zeros((Q_LORA_W, HEADS, pad), w_uq.dtype)
    wq_a = jnp.concatenate([wq, qz], axis=2).reshape(Q_LORA_W, HEADS * HEAD_SLAB)
    wq_b = jnp.concatenate([jnp.zeros((Q_LORA_W, HEADS, QK_NOPE_W), w_uq.dtype),
                            wq[:, :, QK_NOPE_W:][:, :, partner], qz], axis=2).reshape(Q_LORA_W, HEADS * HEAD_SLAB)
    wkv = w_ukv.reshape(KV_LORA_W, HEADS, QK_NOPE_W + V_HEAD_W)
    wk = jnp.concatenate([wkv[:, :, :QK_NOPE_W], jnp.zeros((KV_LORA_W, HEADS, HEAD_SLAB - QK_NOPE_W), w_ukv.dtype)],
                         axis=2).reshape(KV_LORA_W, HEADS * HEAD_SLAB)
    wv = wkv[:, :, QK_NOPE_W:].reshape(KV_LORA_W, HEADS * V_HEAD_W)
    cast = lambda a: a.astype(MXU_DTYPE)
    return {"w_in": cast(w_in_aug), "q_g": q_g.reshape(1, -1), "kv_g": kv_g.reshape(1, -1),
            "wq_a_t": cast(wq_a.T), "wq_b_t": cast(wq_b.T), "wk": cast(wk), "wv_t": cast(wv.T)}


def _heads_and_residual(x, att, f, u, hy, gate, w_out):
    conv_w, conv_b, d_bias = hy[0], hy[1], hy[8]
    fm = _fourier_mix(f)
    x0, z = _hyena_pre(u, conv_w, conv_b)
    kf = _hyena_spectrum(_hyena_filter_taps(x.shape[1], hy))
    y = _hyena_conv(z, kf)
    return _out_proj(x, att, fm, x0, y, z, d_bias, gate, w_out)


def kernel(x, c, ctx, c_ctx, norm1_g, norm2_g, w_mod, b_mod, w_in, q_norm_g, kv_norm_g, w_uq, w_ukv,
           hy_conv_w, hy_conv_b, hy_w1, hy_b1, hy_freq, hy_w2, hy_b2, hy_w3, hy_d, w_out, w_mlp1,
           w_mlp2, final_norm_g):
    b, l, d = x.shape
    lc = ctx.shape[1]
    depth = w_mod.shape[0]
    rows = 16
    cc = jnp.concatenate([c, c_ctx[None, :], jnp.zeros((rows - b - 1, d), F32)], axis=0)
    mod = _modulation(cc, w_mod, b_mod)
    rope_x = _rope_slabs(l, True)
    rope_c = _rope_slabs(lc, False)
    final_g = final_norm_g.reshape(1, d)
    xc = ctx
    for li in range(depth):
        last = li == depth - 1
        mx = mod[li, :b].reshape(b, 1, N_MODULATION, d)
        mc = jnp.broadcast_to(mod[li, b].reshape(1, 1, N_MODULATION, d), (b, 1, N_MODULATION, d))
        sh1, sc1, g1, sh2, sc2, g2 = [mx[:, :, i] for i in range(N_MODULATION)]
        csh1, csc1, cg1, csh2, csc2, cg2 = [mc[:, :, i] for i in range(N_MODULATION)]
        hy = (hy_conv_w[li], hy_conv_b[li], hy_w1[li], hy_b1[li], hy_freq[li], hy_w2[li], hy_b2[li],
              hy_w3[li], hy_d[li])
        lw = _layer_weights(w_in[li], q_norm_g[li], kv_norm_g[li], w_uq[li], w_ukv[li])
        n1g = norm1_g[li].reshape(1, d)
        n2g = norm2_g[li].reshape(1, d)
        wo = w_out[li].astype(MXU_DTYPE)
        w1 = w_mlp1[li].astype(MXU_DTYPE)
        w2 = w_mlp2[li].astype(MXU_DTYPE)

        q_x, k_x, v_x, f_x, u_x = _in_proj(x, sh1, sc1, n1g, lw, rope_x)
        q_c, k_c, v_c, f_c, u_c = _in_proj(xc, csh1, csc1, n1g, lw, rope_c)
        att_x = _attention(q_x, [(k_x, v_x), (k_c, v_c)])
        x = _heads_and_residual(x, att_x, f_x, u_x, hy, g1, wo)
        x = _mlp(x, sh2, sc2, g2, n2g, w1, w2, final_g, last)
        if not last:
            att_c = _attention(q_c, [(k_c, v_c)])
            xc = _heads_and_residual(xc, att_c, f_c, u_c, hy, cg1, wo)
            xc = _mlp(xc, csh2, csc2, cg2, n2g, w1, w2, final_g, False)
    return x
```

```python
import functools
import math

import jax
import jax.numpy as jnp
from jax import lax
from jax.experimental import pallas as pl
from jax.experimental.pallas import tpu as pltpu

F32 = jnp.float32
MXU_DTYPE = jnp.bfloat16

HEADS = 8
QK_NOPE_W = 64
QK_ROPE_W = 32
V_HEAD_W = 64
Q_LORA_W = 384
KV_LORA_W = 256
AXIS_ROPE_W = QK_ROPE_W // 2
ROPE_THETA = 10000.0
ROPE_GRID_W = 64
FNET_GROUP = 64
FNET_WIDTH = 256
HYENA_WIDTH = 256
N_MODULATION = 6
NORM_EPS = 1e-6
DECAY_TARGET_VAL = 1e-2
FAST_DECAY = 0.3
SLOW_DECAY = 1.5

HEAD_SLAB = 128
V_ONES_ROWS = 16
V_SLAB = V_HEAD_W + V_ONES_ROWS
OFF_KV_LORA = Q_LORA_W
OFF_K_ROPE = OFF_KV_LORA + KV_LORA_W
OFF_FNET = OFF_K_ROPE + QK_ROPE_W
OFF_HYENA = OFF_FNET + FNET_WIDTH

AUG_KR = OFF_K_ROPE
AUG_KR_SWAP = AUG_KR + HEAD_SLAB
AUG_F = AUG_KR_SWAP + HEAD_SLAB
AUG_H = AUG_F + FNET_WIDTH
AUG_W = AUG_H + 3 * HYENA_WIDTH

V7X_VMEM_LIMIT = 56 * 1024 * 1024


def _cparams(sem):
    return pltpu.CompilerParams(dimension_semantics=sem, vmem_limit_bytes=V7X_VMEM_LIMIT)


def _dot(a, b):
    return jnp.dot(a, b, preferred_element_type=F32)


def _rms(x, g):
    return x * lax.rsqrt(jnp.mean(x * x, axis=-1, keepdims=True) + NORM_EPS) * g


def _const_spec(shape):
    n = len(shape)
    return pl.BlockSpec(shape, lambda *_: (0,) * n)


def _mod_kernel(c_ref, w_ref, b_ref, o_ref):
    c = c_ref[...]
    s = c / (1.0 + jnp.exp(-c))
    o_ref[0] = _dot(s.astype(MXU_DTYPE), w_ref[0].astype(MXU_DTYPE)) + b_ref[0]


def _modulation(cc, w_mod, b_mod):
    depth, d, n = w_mod.shape
    r = cc.shape[0]
    tn = 1024
    return pl.pallas_call(
        _mod_kernel,
        out_shape=jax.ShapeDtypeStruct((depth, r, n), F32),
        grid=(depth, n // tn),
        in_specs=[pl.BlockSpec((r, d), lambda l, j: (0, 0)),
                  pl.BlockSpec((1, d, tn), lambda l, j: (l, 0, j)),
                  pl.BlockSpec((1, 1, tn), lambda l, j: (l, 0, j))],
        out_specs=pl.BlockSpec((1, r, tn), lambda l, j: (l, 0, j)),
        compiler_params=_cparams(("arbitrary", "arbitrary")),
        name="modulation",
    )(cc, w_mod, b_mod.reshape(depth, 1, n))


def _in_proj_kernel(x_ref, sh_ref, sc_ref, g_ref, win_ref, qg_ref, kvg_ref, wqa_ref, wqb_ref,
                    wk_ref, wv_ref, cos_ref, sin_ref, cos_t_ref, sin_t_ref,
                    qt_ref, k_ref, vt_ref, f_ref, u_ref, *, q_scale):
    h = _rms(x_ref[0], g_ref[...]) * (1.0 + sc_ref[0]) + sh_ref[0]
    p = _dot(h.astype(MXU_DTYPE), win_ref[...])
    cq_t = _rms(p[:, :Q_LORA_W], qg_ref[...]).T.astype(MXU_DTYPE)
    qa_t = _dot(wqa_ref[...], cq_t)
    qb_t = _dot(wqb_ref[...], cq_t)
    cos_t = cos_t_ref[...]
    sin_t = sin_t_ref[...]
    ckv = _rms(p[:, OFF_KV_LORA:OFF_K_ROPE], kvg_ref[...])
    kn = _dot(ckv.astype(MXU_DTYPE), wk_ref[...])
    kr = p[:, AUG_KR:AUG_KR_SWAP] * cos_ref[...] + p[:, AUG_KR_SWAP:AUG_F] * sin_ref[...]
    for hd in range(HEADS):
        hs = slice(hd * HEAD_SLAB, (hd + 1) * HEAD_SLAB)
        qt_ref[0, hs, :] = ((qa_t[hs] * cos_t + qb_t[hs] * sin_t) * q_scale).astype(qt_ref.dtype)
        k_ref[0, :, hs] = (kn[:, hs] + kr).astype(k_ref.dtype)
    vt = _dot(wv_ref[...], ckv.T.astype(MXU_DTYPE)).astype(vt_ref.dtype)
    ones = jnp.ones((V_ONES_ROWS, vt.shape[1]), vt_ref.dtype)
    for hd in range(HEADS):
        vt_ref[0, hd * V_SLAB:hd * V_SLAB + V_HEAD_W, :] = vt[hd * V_HEAD_W:(hd + 1) * V_HEAD_W]
        vt_ref[0, hd * V_SLAB + V_HEAD_W:(hd + 1) * V_SLAB, :] = ones
    _store_rows(f_ref, (0,), slice(None), p[:, AUG_F:AUG_H])
    u_ref[0] = p[:, AUG_H:AUG_W]


def _in_proj(x, shift, scale, norm_g, lw, rope):
    b, l, d = x.shape
    tm = min(512, l)
    q_scale = math.log2(math.e) / math.sqrt(QK_NOPE_W + QK_ROPE_W)
    tok = lambda w: pl.BlockSpec((1, tm, w), lambda bi, i: (bi, i, 0))
    tok_t = lambda w: pl.BlockSpec((1, w, tm), lambda bi, i: (bi, 0, i))
    vec = pl.BlockSpec((1, 1, d), lambda bi, i: (bi, 0, 0))
    tab = pl.BlockSpec((tm, HEAD_SLAB), lambda bi, i: (i, 0))
    tab_t = pl.BlockSpec((HEAD_SLAB, tm), lambda bi, i: (0, i))
    qk_w = HEADS * HEAD_SLAB
    v_w = HEADS * V_HEAD_W
    return pl.pallas_call(
        functools.partial(_in_proj_kernel, q_scale=q_scale),
        out_shape=(jax.ShapeDtypeStruct((b, qk_w, l), MXU_DTYPE),
                   jax.ShapeDtypeStruct((b, l, qk_w), MXU_DTYPE),
                   jax.ShapeDtypeStruct((b, HEADS * V_SLAB, l), MXU_DTYPE),
                   jax.ShapeDtypeStruct((b, FNET_WIDTH // LANE_TILE, l, LANE_TILE), F32),
                   jax.ShapeDtypeStruct((b, l, 3 * HYENA_WIDTH), F32)),
        grid=(b, l // tm),
        in_specs=[tok(d), vec, vec, _const_spec((1, d)), _const_spec((d, AUG_W)),
                  _const_spec((1, Q_LORA_W)), _const_spec((1, KV_LORA_W)),
                  _const_spec((qk_w, Q_LORA_W)), _const_spec((qk_w, Q_LORA_W)),
                  _const_spec((KV_LORA_W, qk_w)), _const_spec((v_w, KV_LORA_W)), tab, tab, tab_t, tab_t],
        out_specs=(tok_t(qk_w), tok(qk_w), tok_t(HEADS * V_SLAB),
                   pl.BlockSpec((1, FNET_WIDTH // LANE_TILE, tm, LANE_TILE), lambda bi, i: (bi, 0, i, 0)),
                   tok(3 * HYENA_WIDTH)),
        compiler_params=_cparams(("parallel", "arbitrary")),
        name="in_proj",
    )(x, shift, scale, norm_g, lw["w_in"], lw["q_g"], lw["kv_g"], lw["wq_a_t"], lw["wq_b_t"],
      lw["wk"], lw["wv_t"], *rope)


def _attn_kernel(*refs, nseg, tk):
    qt_ref = refs[0]
    segs = [(refs[1 + 2 * s], refs[2 + 2 * s]) for s in range(nseg)]
    o_ref, s_ref, p_ref = refs[1 + 2 * nseg:]
    tq = qt_ref.shape[2]
    s_total = s_ref.shape[1]

    def scores(hd):
        qt = qt_ref[0, hd * HEAD_SLAB:(hd + 1) * HEAD_SLAB, :]
        mx = jnp.full((1, tq), -jnp.inf, F32)
        off = 0
        for k_ref, _ in segs:
            s_len = k_ref.shape[1]
            for lo in range(0, s_len, tk):
                hi = min(lo + tk, s_len)
                s = _dot(k_ref[0, lo:hi, hd * HEAD_SLAB:(hd + 1) * HEAD_SLAB], qt)
                s_ref[hd, off + lo:off + hi, :] = s
                mx = jnp.maximum(mx, jnp.max(s, axis=0, keepdims=True))
            off += s_len
        return mx

    def probs(hd, mx):
        for lo in range(0, s_total, tk):
            hi = min(lo + tk, s_total)
            p_ref[hd, lo:hi, :] = jnp.exp2(s_ref[hd, lo:hi, :] - mx).astype(p_ref.dtype)

    def values(hd):
        acc = jnp.zeros((V_SLAB, tq), F32)
        off = 0
        for _, vt_ref in segs:
            s_len = vt_ref.shape[2]
            acc = acc + _dot(vt_ref[0, hd * V_SLAB:(hd + 1) * V_SLAB, :], p_ref[hd, off:off + s_len, :])
            off += s_len
        return acc[:V_HEAD_W] / acc[V_HEAD_W:V_HEAD_W + 1]

    mx0 = scores(0)
    probs(0, mx0)
    mx1 = scores(1)
    o0 = values(0)
    probs(1, mx1)
    o1 = values(1)
    o_ref[0] = jnp.concatenate([o0, o1], axis=0).T.astype(o_ref.dtype)


def _attention(qt, kv_segs):
    b, _, l = qt.shape
    tq = min(512, l)
    pair_qk = 2 * HEAD_SLAB
    pair_v = 2 * V_HEAD_W
    in_specs = [pl.BlockSpec((1, pair_qk, tq), lambda bi, hp, i: (bi, hp, i))]
    args = [qt]
    for k, vt in kv_segs:
        s_len = k.shape[1]
        in_specs.append(pl.BlockSpec((1, s_len, pair_qk), lambda bi, hp, i: (bi, 0, hp)))
        in_specs.append(pl.BlockSpec((1, 2 * V_SLAB, s_len), lambda bi, hp, i: (bi, hp, 0)))
        args += [k, vt]
    s_total = sum(k.shape[1] for k, _ in kv_segs)
    return pl.pallas_call(
        functools.partial(_attn_kernel, nseg=len(kv_segs), tk=256),
        out_shape=jax.ShapeDtypeStruct((b, l, HEADS * V_HEAD_W), MXU_DTYPE),
        grid=(b, HEADS // 2, l // tq),
        in_specs=in_specs,
        out_specs=pl.BlockSpec((1, tq, pair_v), lambda bi, hp, i: (bi, i, hp)),
        scratch_shapes=[pltpu.VMEM((2, s_total, tq), F32), pltpu.VMEM((2, s_total, tq), MXU_DTYPE)],
        compiler_params=_cparams(("parallel", "parallel", "arbitrary")),
        name="attention",
    )(*args)


def _split(n, zero_padded):
    n1 = 1 << (int(math.log2(n)) // 2)
    n2 = n // n1
    if zero_padded and n1 // 2 < 16 and n2 > n1:
        n1, n2 = n2, n1
    return n1, n2


def _angles(rows, cols, n):
    idx = (rows[:, None] * cols[None, :]) % n
    return idx.astype(F32) * (2.0 * math.pi / n)


def _coarse_fwd_table(n1, k_in, k1_count):
    th = _angles(jnp.arange(k1_count, dtype=jnp.int32), jnp.arange(k_in, dtype=jnp.int32), n1)
    return jnp.concatenate([jnp.cos(th), -jnp.sin(th)], axis=0)


def _coarse_inv_table(n1, t_out, n, k1_count):
    k1 = jnp.arange(k1_count, dtype=jnp.int32)
    th = _angles(jnp.arange(t_out, dtype=jnp.int32), k1, n1)
    w = jnp.where((k1 == 0) | (k1 == n1 // 2), 1.0, jnp.where(k1 < n1 // 2, 2.0, 0.0)) * (1.0 / n)
    return jnp.concatenate([jnp.cos(th) * w, -jnp.sin(th) * w], axis=1)


def _fine_tables(n1, n2, k1_count):
    n = n1 * n2
    k = (jnp.arange(k1_count, dtype=jnp.int32)[:, None] + n1 * jnp.arange(n2, dtype=jnp.int32)[None, :])
    idx = (k[:, :, None] * jnp.arange(n2, dtype=jnp.int32)[None, None, :]) % n
    ph = idx.astype(F32) * (2.0 * math.pi / n)
    c, s = jnp.cos(ph), jnp.sin(ph)
    fwd = jnp.concatenate([jnp.concatenate([c, s], axis=2), jnp.concatenate([-s, c], axis=2)], axis=1)
    return fwd, jnp.swapaxes(fwd, 1, 2)


def _hermitian_planes(n1):
    return -(-(n1 // 2 + 1) // 8) * 8


PLANE_PAD_ROWS = 8


def _lmm_kernel(t_ref, x_ref, o_ref):
    o_ref[0] = _dot(t_ref[...], x_ref[0].astype(MXU_DTYPE)).astype(o_ref.dtype)


def _left_matmul(table, x, out_dtype):
    b, k, n = x.shape
    m = table.shape[0]
    tc = min(2048, n)
    return pl.pallas_call(
        _lmm_kernel,
        out_shape=jax.ShapeDtypeStruct((b, m, n), out_dtype),
        grid=(b, n // tc),
        in_specs=[_const_spec((m, k)), pl.BlockSpec((1, k, tc), lambda bi, j: (bi, 0, j))],
        out_specs=pl.BlockSpec((1, m, tc), lambda bi, j: (bi, 0, j)),
        compiler_params=_cparams(("parallel", "arbitrary")),
        name="dft_coarse",
    )(table.astype(MXU_DTYPE), x)


def _kf_mid_kernel(x_ref, g_ref, o_ref):
    n2 = x_ref.shape[3]
    c = o_ref.shape[2]
    for j in range(x_ref.shape[2]):
        zf = _dot(g_ref[j], x_ref[0, :, j].reshape(2 * n2, x_ref.shape[4]))
        o_ref[j, :n2] = (zf[:n2, :c] + zf[:n2, c:]).astype(o_ref.dtype)
        o_ref[j, n2:] = (zf[n2:, :c] - zf[n2:, c:]).astype(o_ref.dtype)


def _kf_mid(a5, g):
    _, _, k1p, n2, c2 = a5.shape
    c = c2 // 2
    t1 = 8
    return pl.pallas_call(
        _kf_mid_kernel,
        out_shape=jax.ShapeDtypeStruct((k1p, 2 * n2, c), MXU_DTYPE),
        grid=(k1p // t1,),
        in_specs=[pl.BlockSpec((1, 2, t1, n2, c2), lambda i: (0, 0, i, 0, 0)),
                  pl.BlockSpec((t1, 2 * n2, 2 * n2), lambda i: (i, 0, 0))],
        out_specs=pl.BlockSpec((t1, 2 * n2, c), lambda i: (i, 0, 0)),
        compiler_params=_cparams(("arbitrary",)),
        name="hyena_filter_spectrum",
    )(a5, g)


LANE_TILE = 128


def _split_lanes(a):
    return jnp.stack([a[..., i * LANE_TILE:(i + 1) * LANE_TILE] for i in range(a.shape[-1] // LANE_TILE)], axis=-3)


def _load_rows(ref, pre, rows):
    return jnp.concatenate([ref[pre + (t, rows, slice(None))] for t in range(ref.shape[len(pre)])], axis=1)


def _store_rows(ref, pre, rows, val):
    for t in range(ref.shape[len(pre)]):
        ref[pre + (t, rows, slice(None))] = val[:, t * LANE_TILE:(t + 1) * LANE_TILE]


def _plane_rows(plane, sp, n2):
    return pl.ds(pl.multiple_of(plane * sp, 8), n2)


def _fnet_kernel(f_ref, fa_ref, g_ref, cm_ref, o_ref, a_ref, r_ref, *, n2):
    l = f_ref.shape[2]
    n1 = l // n2
    sp = n2 + PLANE_PAD_ROWS
    fa = fa_ref[...]

    def coarse(j, carry):
        zs = _load_rows(f_ref, (0,), pl.ds(j, n1, stride=n2)).astype(MXU_DTYPE)
        _store_rows(a_ref, (), pl.ds(j, 2 * n1, stride=sp), _dot(fa, zs))
        return carry

    lax.fori_loop(0, n2, coarse, 0, unroll=4)

    def fine(k1, carry):
        x = jnp.concatenate([_load_rows(a_ref, (), _plane_rows(k1, sp, n2)),
                             _load_rows(a_ref, (), _plane_rows(n1 + k1, sp, n2))], axis=0)
        pf = _dot(g_ref[k1], x.astype(MXU_DTYPE)).astype(MXU_DTYPE)
        res = _dot(pf[:n2], cm_ref[0]) + _dot(pf[n2:], cm_ref[1])
        _store_rows(r_ref, (), pl.ds(k1, n2, stride=n1), res)
        return carry

    lax.fori_loop(0, n1, fine, 0, unroll=4)
    o_ref[0] = _load_rows(r_ref, (), slice(None)).astype(o_ref.dtype)


def _fourier_mix(f):
    b, tiles, l, _ = f.shape
    c = tiles * LANE_TILE
    n1, n2 = _split(l, False)
    fa = _coarse_fwd_table(n1, n1, n1).astype(MXU_DTYPE)
    g, _ = _fine_tables(n1, n2, n1)
    ch = jnp.arange(c, dtype=jnp.int32)
    th = _angles(ch % FNET_GROUP, ch % FNET_GROUP, FNET_GROUP)
    same = (ch[:, None] // FNET_GROUP) == (ch[None, :] // FNET_GROUP)
    norm = 1.0 / math.sqrt(l * FNET_GROUP)
    cm = jnp.stack([jnp.where(same, jnp.cos(th), 0.0), jnp.where(same, jnp.sin(th), 0.0)]) * norm
    tiles = c // LANE_TILE
    return pl.pallas_call(
        functools.partial(_fnet_kernel, n2=n2),
        out_shape=jax.ShapeDtypeStruct((b, l, c), MXU_DTYPE),
        grid=(b,),
        in_specs=[pl.BlockSpec((1, tiles, l, LANE_TILE), lambda bi: (bi, 0, 0, 0)),
                  _const_spec(fa.shape), _const_spec(g.shape), _const_spec(cm.shape)],
        out_specs=pl.BlockSpec((1, l, c), lambda bi: (bi, 0, 0)),
        scratch_shapes=[pltpu.VMEM((tiles, 2 * n1 * (n2 + PLANE_PAD_ROWS), LANE_TILE), F32),
                        pltpu.VMEM((tiles, l, LANE_TILE), F32)],
        compiler_params=_cparams(("parallel",)),
        name="fnet",
    )(f, fa, g.astype(MXU_DTYPE), cm.astype(MXU_DTYPE))


def _hy_filter_kernel(ze_ref, w1_ref, b1_ref, fr_ref, w2_ref, b2_ref, w3_ref, t_ref, dl_ref, o_ref):
    hp = lax.Precision.HIGHEST
    fr = fr_ref[...]
    h = jnp.sin(fr * (jnp.dot(ze_ref[...], w1_ref[...], precision=hp, preferred_element_type=F32) + b1_ref[...]))
    h = jnp.sin(fr * (jnp.dot(h, w2_ref[...], precision=hp, preferred_element_type=F32) + b2_ref[...]))
    h = jnp.dot(h, w3_ref[...], precision=hp, preferred_element_type=F32)
    c = dl_ref.shape[1]
    decay = jnp.exp(-t_ref[...] * dl_ref[...])
    hf = h[:, :c] * decay
    row = lax.broadcasted_iota(jnp.int32, (h.shape[0], 1), 0)
    hb = jnp.where(row == 0, 0.0, h[:, c:] * decay)
    nrm = jnp.sum(jnp.abs(hf), axis=0, keepdims=True) + jnp.sum(jnp.abs(hb), axis=0, keepdims=True)
    o_ref[:, :c] = hf / nrm
    o_ref[:, c:] = hb / nrm


def _hyena_filter_taps(l, hy):
    conv_w, conv_b, w1, b1, freq, w2, b2, w3, d_bias = hy
    emb = w1.shape[0]
    order = w1.shape[1]
    c = w3.shape[1] // 2
    t = jnp.linspace(0.0, 1.0, l, dtype=F32)[:, None]
    bands = (emb - 1) // 2
    fr = jnp.linspace(1e-4, bands - 1, bands, dtype=F32)
    ang = 2.0 * math.pi * jnp.arange(l, dtype=F32)[:, None] / l * fr
    z = jnp.concatenate([t, jnp.cos(ang), -jnp.sin(ang)], axis=-1)
    emb_pad = 128
    z = jnp.pad(z, ((0, 0), (0, emb_pad - emb)))
    w1p = jnp.pad(w1, ((0, emb_pad - emb), (0, 0)))
    min_decay = math.log(DECAY_TARGET_VAL) / SLOW_DECAY
    max_decay = math.log(DECAY_TARGET_VAL) / FAST_DECAY
    deltas = jnp.abs(jnp.linspace(min_decay, max_decay, c, dtype=F32))[None, :]
    args = (z, w1p, b1.reshape(1, order), freq.reshape(1, order), w2, b2.reshape(1, order), w3, t, deltas)
    return pl.pallas_call(
        _hy_filter_kernel,
        out_shape=jax.ShapeDtypeStruct((l, 2 * c), F32),
        grid=(1,),
        in_specs=[_const_spec(a.shape) for a in args],
        out_specs=_const_spec((l, 2 * c)),
        compiler_params=_cparams(("arbitrary",)),
        name="hyena_filter",
    )(*args)


def _hy_pre_kernel(u_ref, prev_ref, next_ref, w_ref, b_ref, x0_ref, z_ref):
    i = pl.program_id(1)
    tl = u_ref.shape[1]
    c = u_ref.shape[2] // 3
    u = u_ref[0]
    row = lax.broadcasted_iota(jnp.int32, (tl, 1), 0)
    before = jnp.where(i == 0, 0.0, prev_ref[0, 7:8, :])
    after = jnp.where(i == pl.num_programs(1) - 1, 0.0, next_ref[0, 0:1, :])
    um = jnp.where(row == 0, before, pltpu.roll(u, 1, 0))
    up = jnp.where(row == tl - 1, after, pltpu.roll(u, tl - 1, 0))
    out = b_ref[...] + um * w_ref[0:1, :] + u * w_ref[1:2, :] + up * w_ref[2:3, :]
    _store_rows(x0_ref, (0,), slice(None), out[:, :c])
    _store_rows(z_ref, (0,), slice(None), out[:, c:2 * c] * out[:, 2 * c:])


def _hyena_pre(u, conv_w, conv_b):
    b, l, w = u.shape
    c = w // 3
    tl = min(512, l)
    nb8 = tl // 8
    last8 = l // 8 - 1
    tok = lambda width: pl.BlockSpec((1, tl, width), lambda bi, i: (bi, i, 0))
    tiles = c // LANE_TILE
    split = pl.BlockSpec((1, tiles, tl, LANE_TILE), lambda bi, i: (bi, 0, i, 0))
    split_shape = jax.ShapeDtypeStruct((b, tiles, l, LANE_TILE), F32)
    return pl.pallas_call(
        _hy_pre_kernel,
        out_shape=(split_shape, split_shape),
        grid=(b, l // tl),
        in_specs=[tok(w),
                  pl.BlockSpec((1, 8, w), lambda bi, i: (bi, jnp.maximum(i * nb8 - 1, 0), 0)),
                  pl.BlockSpec((1, 8, w), lambda bi, i: (bi, jnp.minimum((i + 1) * nb8, last8), 0)),
                  _const_spec(conv_w.shape), _const_spec((1, w))],
        out_specs=(split, split),
        compiler_params=_cparams(("parallel", "arbitrary")),
        name="hyena_pre",
    )(u, u, u, conv_w, conv_b.reshape(1, w))


def _hyena_spectrum(taps):
    l, c2 = taps.shape
    n = 2 * l
    n1, n2 = _split(n, True)
    k1p = _hermitian_planes(n1)
    a = _left_matmul(_coarse_fwd_table(n1, n1 // 2, k1p), taps.reshape(1, n1 // 2, n2 * c2), MXU_DTYPE)
    g, _ = _fine_tables(n1, n2, k1p)
    return _kf_mid(a.reshape(1, 2, k1p, n2, c2), g.astype(MXU_DTYPE))


def _hyena_kernel(x0_ref, z_ref, d_ref, fa_ref, g_ref, h_ref, kf_ref, fai_ref, o_ref, a_ref, y_ref, *, n2):
    l = z_ref.shape[2]
    n1h = l // n2
    k1p = g_ref.shape[0]
    sp = n2 + PLANE_PAD_ROWS
    fa = fa_ref[...]
    fai = fai_ref[...]

    def coarse(j, carry):
        zs = _load_rows(z_ref, (0,), pl.ds(j, n1h, stride=n2)).astype(MXU_DTYPE)
        _store_rows(a_ref, (), pl.ds(j, 2 * k1p, stride=sp), _dot(fa, zs))
        return carry

    lax.fori_loop(0, n2, coarse, 0, unroll=4)

    def fine(k1, carry):
        re_rows = _plane_rows(k1, sp, n2)
        im_rows = _plane_rows(k1p + k1, sp, n2)
        x = jnp.concatenate([_load_rows(a_ref, (), re_rows), _load_rows(a_ref, (), im_rows)], axis=0)
        zf = _dot(g_ref[k1], x.astype(MXU_DTYPE))
        zr, zi = zf[:n2], zf[n2:]
        kr = kf_ref[k1, :n2].astype(F32)
        ki = kf_ref[k1, n2:].astype(F32)
        yf = jnp.concatenate([zr * kr - zi * ki, zr * ki + zi * kr], axis=0).astype(MXU_DTYPE)
        bk = _dot(h_ref[k1], yf)
        _store_rows(a_ref, (), re_rows, bk[:n2])
        _store_rows(a_ref, (), im_rows, bk[n2:])
        return carry

    lax.fori_loop(0, k1p, fine, 0, unroll=4)

    def coarse_inv(j, carry):
        bs = _load_rows(a_ref, (), pl.ds(j, 2 * k1p, stride=sp)).astype(MXU_DTYPE)
        _store_rows(y_ref, (), pl.ds(j, n1h, stride=n2), _dot(fai, bs))
        return carry

    lax.fori_loop(0, n2, coarse_inv, 0, unroll=4)
    everything = slice(None)
    gated = _load_rows(x0_ref, (0,), everything) * (
        _load_rows(y_ref, (), everything) + d_ref[...] * _load_rows(z_ref, (0,), everything))
    o_ref[0] = gated.astype(o_ref.dtype)


def _hyena_gated_conv(x0, z, d_bias, kf):
    b, tiles, l, _ = z.shape
    c = tiles * LANE_TILE
    n = 2 * l
    n1, n2 = _split(n, True)
    k1p = kf.shape[0]
    fa = _coarse_fwd_table(n1, n1 // 2, k1p).astype(MXU_DTYPE)
    fai = _coarse_inv_table(n1, n1 // 2, n, k1p).astype(MXU_DTYPE)
    g, h = _fine_tables(n1, n2, k1p)
    seq = pl.BlockSpec((1, tiles, l, LANE_TILE), lambda bi: (bi, 0, 0, 0), pipeline_mode=pl.Buffered(1))
    resident = lambda a: pl.BlockSpec(a.shape, lambda bi: (0,) * a.ndim, pipeline_mode=pl.Buffered(1))
    return pl.pallas_call(
        functools.partial(_hyena_kernel, n2=n2),
        out_shape=jax.ShapeDtypeStruct((b, l, c), MXU_DTYPE),
        grid=(b,),
        in_specs=[seq, seq, _const_spec((1, c)), _const_spec(fa.shape), resident(g), resident(h), resident(kf),
                  _const_spec(fai.shape)],
        out_specs=pl.BlockSpec((1, l, c), lambda bi: (bi, 0, 0)),
        scratch_shapes=[pltpu.VMEM((tiles, 2 * k1p * (n2 + PLANE_PAD_ROWS), LANE_TILE), F32),
                        pltpu.VMEM((tiles, l, LANE_TILE), F32)],
        compiler_params=_cparams(("parallel",)),
        name="hyena_conv",
    )(x0, z, d_bias.reshape(1, c), fa, g.astype(MXU_DTYPE), h.astype(MXU_DTYPE), kf, fai)


def _out_proj_kernel(x_ref, att_ref, f_ref, hz_ref, gate_ref, w_ref, o_ref):
    a_w = att_ref.shape[2]
    f_w = f_ref.shape[2]
    mix = (_dot(att_ref[0], w_ref[:a_w]) + _dot(f_ref[0], w_ref[a_w:a_w + f_w])
           + _dot(hz_ref[0], w_ref[a_w + f_w:]))
    o_ref[0] = x_ref[0] + gate_ref[0] * mix


def _out_proj(x, att, f, hz, gate, w_out):
    b, l, d = x.shape
    tm = min(512, l)
    tok = lambda w: pl.BlockSpec((1, tm, w), lambda bi, i: (bi, i, 0))
    return pl.pallas_call(
        _out_proj_kernel,
        out_shape=jax.ShapeDtypeStruct(x.shape, F32),
        grid=(b, l // tm),
        in_specs=[tok(d), tok(att.shape[2]), tok(f.shape[2]), tok(hz.shape[2]),
                  pl.BlockSpec((1, 1, d), lambda bi, i: (bi, 0, 0)), _const_spec(w_out.shape)],
        out_specs=tok(d),
        compiler_params=_cparams(("parallel", "arbitrary")),
        name="out_proj",
    )(x, att, f, hz, gate, w_out)


def _mlp_kernel(x_ref, sh_ref, sc_ref, gate_ref, g_ref, w1_ref, w2_ref, fg_ref, o_ref, *, chunk, final_norm):
    x = x_ref[0]
    h = (_rms(x, g_ref[...]) * (1.0 + sc_ref[0]) + sh_ref[0]).astype(MXU_DTYPE)
    acc = jnp.zeros(x.shape, F32)
    for c in range(w1_ref.shape[1] // chunk):
        a = jnp.maximum(_dot(h, w1_ref[:, c * chunk:(c + 1) * chunk]), 0.0)
        acc = acc + _dot((a * a).astype(MXU_DTYPE), w2_ref[c * chunk:(c + 1) * chunk, :])
    out = x + gate_ref[0] * acc
    if final_norm:
        out = _rms(out, fg_ref[...])
    o_ref[0] = out


def _mlp(x, shift, scale, gate, norm_g, w1, w2, final_g, final_norm):
    b, l, d = x.shape
    tm = min(512, l)
    tok = pl.BlockSpec((1, tm, d), lambda bi, i: (bi, i, 0))
    vec = pl.BlockSpec((1, 1, d), lambda bi, i: (bi, 0, 0))
    resident = lambda shape: pl.BlockSpec(shape, lambda bi, i: (0, 0), pipeline_mode=pl.Buffered(1))
    return pl.pallas_call(
        functools.partial(_mlp_kernel, chunk=512, final_norm=final_norm),
        out_shape=jax.ShapeDtypeStruct(x.shape, F32),
        grid=(b, l // tm),
        in_specs=[tok, vec, vec, vec, _const_spec((1, d)), resident(w1.shape), resident(w2.shape),
                  _const_spec((1, d))],
        out_specs=tok,
        compiler_params=_cparams(("parallel", "arbitrary")),
        name="mlp",
    )(x, shift, scale, gate, norm_g, w1, w2, final_g)


def _rope_partner_perm():
    half = AXIS_ROPE_W // 2
    partner, sign = [], []
    for j in range(QK_ROPE_W):
        first = (j % AXIS_ROPE_W) < half
        partner.append(j + half if first else j - half)
        sign.append(-1.0 if first else 1.0)
    return partner, sign


def _rope_slabs(n_tokens, rotary):
    partner, sign = _rope_partner_perm()
    if rotary:
        rows = n_tokens // ROPE_GRID_W
        row = jnp.repeat(jnp.arange(rows, dtype=F32), ROPE_GRID_W)
        col = jnp.tile(jnp.arange(ROPE_GRID_W, dtype=F32), rows)
        inv = ROPE_THETA ** (-jnp.arange(0, AXIS_ROPE_W, 2, dtype=F32) / AXIS_ROPE_W)
        ang = jnp.concatenate([row[:, None] * inv, col[:, None] * inv], axis=-1)
        cos16, sin16 = jnp.cos(ang), jnp.sin(ang)
        half = AXIS_ROPE_W // 2
        cols = jnp.array([(j // AXIS_ROPE_W) * half + j % half for j in range(QK_ROPE_W)])
        cos32 = cos16[:, cols]
        sin32 = sin16[:, cols] * jnp.array(sign, F32)
    else:
        cos32 = jnp.ones((n_tokens, QK_ROPE_W), F32)
        sin32 = jnp.zeros((n_tokens, QK_ROPE_W), F32)
    pad = HEAD_SLAB - QK_NOPE_W - QK_ROPE_W
    cos_t = jnp.concatenate([jnp.ones((n_tokens, QK_NOPE_W), F32), cos32, jnp.zeros((n_tokens, pad), F32)], axis=1)
    sin_t = jnp.concatenate([jnp.zeros((n_tokens, QK_NOPE_W), F32), sin32, jnp.zeros((n_tokens, pad), F32)], axis=1)
    return cos_t, sin_t, cos_t.T, sin_t.T


def _layer_weights(w_in, q_g, kv_g, w_uq, w_ukv):
    partner, _ = _rope_partner_perm()
    partner = jnp.array(partner)
    d = w_in.shape[0]
    pad = HEAD_SLAB - QK_NOPE_W - QK_ROPE_W
    zeros = lambda rows, w: jnp.zeros((rows, w), w_in.dtype)
    kr = w_in[:, OFF_K_ROPE:OFF_FNET]
    w_in_aug = jnp.concatenate([
        w_in[:, :OFF_K_ROPE],
        zeros(d, QK_NOPE_W), kr, zeros(d, pad),
        zeros(d, QK_NOPE_W), kr[:, partner], zeros(d, pad),
        w_in[:, OFF_FNET:]], axis=1)
    wq = w_uq.reshape(Q_LORA_W, HEADS, QK_NOPE_W + QK_ROPE_W)
    qz = jnp.zeros((Q_LORA_W, HEADS, pad), w_uq.dtype)
    wq_a = jnp.concatenate([wq, qz], axis=2).reshape(Q_LORA_W, HEADS * HEAD_SLAB)
    wq_b = jnp.concatenate([jnp.zeros((Q_LORA_W, HEADS, QK_NOPE_W), w_uq.dtype),
                            wq[:, :, QK_NOPE_W:][:, :, partner], qz], axis=2).reshape(Q_LORA_W, HEADS * HEAD_SLAB)
    wkv = w_ukv.reshape(KV_LORA_W, HEADS, QK_NOPE_W + V_HEAD_W)
    wk = jnp.concatenate([wkv[:, :, :QK_NOPE_W], jnp.zeros((KV_LORA_W, HEADS, HEAD_SLAB - QK_NOPE_W), w_ukv.dtype)],
                         axis=2).reshape(KV_LORA_W, HEADS * HEAD_SLAB)
    wv = wkv[:, :, QK_NOPE_W:].reshape(KV_LORA_W, HEADS * V_HEAD_W)
    cast = lambda a: a.astype(MXU_DTYPE)
    return {"w_in": cast(w_in_aug), "q_g": q_g.reshape(1, -1), "kv_g": kv_g.reshape(1, -1),
            "wq_a_t": cast(wq_a.T), "wq_b_t": cast(wq_b.T), "wk": cast(wk), "wv_t": cast(wv.T)}


def _heads_and_residual(x, att, f, u, hy, gate, w_out):
    conv_w, conv_b, d_bias = hy[0], hy[1], hy[8]
    fm = _fourier_mix(f)
    x0, z = _hyena_pre(u, conv_w, conv_b)
    kf = _hyena_spectrum(_hyena_filter_taps(x.shape[1], hy))
    hz = _hyena_gated_conv(x0, z, d_bias, kf)
    return _out_proj(x, att, fm, hz, gate, w_out)


def kernel(x, c, ctx, c_ctx, norm1_g, norm2_g, w_mod, b_mod, w_in, q_norm_g, kv_norm_g, w_uq, w_ukv,
           hy_conv_w, hy_conv_b, hy_w1, hy_b1, hy_freq, hy_w2, hy_b2, hy_w3, hy_d, w_out, w_mlp1,
           w_mlp2, final_norm_g):
    b, l, d = x.shape
    lc = ctx.shape[1]
    depth = w_mod.shape[0]
    rows = 16
    cc = jnp.concatenate([c, c_ctx[None, :], jnp.zeros((rows - b - 1, d), F32)], axis=0)
    mod = _modulation(cc, w_mod, b_mod)
    rope_x = _rope_slabs(l, True)
    rope_c = _rope_slabs(lc, False)
    final_g = final_norm_g.reshape(1, d)
    xc = ctx
    for li in range(depth):
        last = li == depth - 1
        mx = mod[li, :b].reshape(b, 1, N_MODULATION, d)
        mc = jnp.broadcast_to(mod[li, b].reshape(1, 1, N_MODULATION, d), (b, 1, N_MODULATION, d))
        sh1, sc1, g1, sh2, sc2, g2 = [mx[:, :, i] for i in range(N_MODULATION)]
        csh1, csc1, cg1, csh2, csc2, cg2 = [mc[:, :, i] for i in range(N_MODULATION)]
        hy = (hy_conv_w[li], hy_conv_b[li], hy_w1[li], hy_b1[li], hy_freq[li], hy_w2[li], hy_b2[li],
              hy_w3[li], hy_d[li])
        lw = _layer_weights(w_in[li], q_norm_g[li], kv_norm_g[li], w_uq[li], w_ukv[li])
        n1g = norm1_g[li].reshape(1, d)
        n2g = norm2_g[li].reshape(1, d)
        wo = w_out[li].astype(MXU_DTYPE)
        w1 = w_mlp1[li].astype(MXU_DTYPE)
        w2 = w_mlp2[li].astype(MXU_DTYPE)

        q_x, k_x, v_x, f_x, u_x = _in_proj(x, sh1, sc1, n1g, lw, rope_x)
        q_c, k_c, v_c, f_c, u_c = _in_proj(xc, csh1, csc1, n1g, lw, rope_c)
        att_x = _attention(q_x, [(k_x, v_x), (k_c, v_c)])
        x = _heads_and_residual(x, att_x, f_x, u_x, hy, g1, wo)
        x = _mlp(x, sh2, sc2, g2, n2g, w1, w2, final_g, last)
        if not last:
            att_c = _attention(q_c, [(k_c, v_c)])
            xc = _heads_and_residual(xc, att_c, f_c, u_c, hy, cg1, wo)
            xc = _mlp(xc, csh2, csc2, cg2, n2g, w1, w2, final_g, False)
    return x
```

```python
import functools
import math

import jax
import jax.numpy as jnp
from jax import lax
from jax.experimental import pallas as pl
from jax.experimental.pallas import tpu as pltpu

F32 = jnp.float32
MXU_DTYPE = jnp.bfloat16

HEADS = 8
QK_NOPE_W = 64
QK_ROPE_W = 32
V_HEAD_W = 64
Q_LORA_W = 384
KV_LORA_W = 256
AXIS_ROPE_W = QK_ROPE_W // 2
ROPE_THETA = 10000.0
ROPE_GRID_W = 64
FNET_GROUP = 64
FNET_WIDTH = 256
HYENA_WIDTH = 256
N_MODULATION = 6
NORM_EPS = 1e-6
DECAY_TARGET_VAL = 1e-2
FAST_DECAY = 0.3
SLOW_DECAY = 1.5

HEAD_SLAB = 128
V_ONES_ROWS = 16
V_SLAB = V_HEAD_W + V_ONES_ROWS
OFF_KV_LORA = Q_LORA_W
OFF_K_ROPE = OFF_KV_LORA + KV_LORA_W
OFF_FNET = OFF_K_ROPE + QK_ROPE_W
OFF_HYENA = OFF_FNET + FNET_WIDTH

AUG_KR = OFF_K_ROPE
AUG_KR_SWAP = AUG_KR + HEAD_SLAB
AUG_F = AUG_KR_SWAP + HEAD_SLAB
AUG_H = AUG_F + FNET_WIDTH
AUG_W = AUG_H + 3 * HYENA_WIDTH

V7X_VMEM_LIMIT = 56 * 1024 * 1024


def _cparams(sem):
    return pltpu.CompilerParams(dimension_semantics=sem, vmem_limit_bytes=V7X_VMEM_LIMIT)


def _dot(a, b):
    return jnp.dot(a, b, preferred_element_type=F32)


def _rms(x, g):
    return x * lax.rsqrt(jnp.mean(x * x, axis=-1, keepdims=True) + NORM_EPS) * g


def _const_spec(shape):
    n = len(shape)
    return pl.BlockSpec(shape, lambda *_: (0,) * n)


def _mod_kernel(c_ref, w_ref, b_ref, o_ref):
    c = c_ref[...]
    s = c / (1.0 + jnp.exp(-c))
    o_ref[0] = _dot(s.astype(MXU_DTYPE), w_ref[0].astype(MXU_DTYPE)) + b_ref[0]


def _modulation(cc, w_mod, b_mod):
    depth, d, n = w_mod.shape
    r = cc.shape[0]
    tn = 1024
    return pl.pallas_call(
        _mod_kernel,
        out_shape=jax.ShapeDtypeStruct((depth, r, n), F32),
        grid=(depth, n // tn),
        in_specs=[pl.BlockSpec((r, d), lambda l, j: (0, 0)),
                  pl.BlockSpec((1, d, tn), lambda l, j: (l, 0, j)),
                  pl.BlockSpec((1, 1, tn), lambda l, j: (l, 0, j))],
        out_specs=pl.BlockSpec((1, r, tn), lambda l, j: (l, 0, j)),
        compiler_params=_cparams(("arbitrary", "arbitrary")),
        name="modulation",
    )(cc, w_mod, b_mod.reshape(depth, 1, n))


def _in_proj_kernel(x_ref, sh_ref, sc_ref, g_ref, win_ref, qg_ref, kvg_ref, wqa_ref, wqb_ref,
                    wk_ref, wv_ref, cos_ref, sin_ref, cos_t_ref, sin_t_ref,
                    qt_ref, k_ref, vt_ref, f_ref, u_ref, *, q_scale):
    h = _rms(x_ref[0], g_ref[...]) * (1.0 + sc_ref[0]) + sh_ref[0]
    p = _dot(h.astype(MXU_DTYPE), win_ref[...])
    cq_t = _rms(p[:, :Q_LORA_W], qg_ref[...]).T.astype(MXU_DTYPE)
    qa_t = _dot(wqa_ref[...], cq_t)
    qb_t = _dot(wqb_ref[...], cq_t)
    cos_t = cos_t_ref[...]
    sin_t = sin_t_ref[...]
    ckv = _rms(p[:, OFF_KV_LORA:OFF_K_ROPE], kvg_ref[...])
    kn = _dot(ckv.astype(MXU_DTYPE), wk_ref[...])
    kr = p[:, AUG_KR:AUG_KR_SWAP] * cos_ref[...] + p[:, AUG_KR_SWAP:AUG_F] * sin_ref[...]
    for hd in range(HEADS):
        hs = slice(hd * HEAD_SLAB, (hd + 1) * HEAD_SLAB)
        qt_ref[0, hs, :] = ((qa_t[hs] * cos_t + qb_t[hs] * sin_t) * q_scale).astype(qt_ref.dtype)
        k_ref[0, :, hs] = (kn[:, hs] + kr).astype(k_ref.dtype)
    vt = _dot(wv_ref[...], ckv.T.astype(MXU_DTYPE)).astype(vt_ref.dtype)
    ones = jnp.ones((V_ONES_ROWS, vt.shape[1]), vt_ref.dtype)
    for hd in range(HEADS):
        vt_ref[0, hd * V_SLAB:hd * V_SLAB + V_HEAD_W, :] = vt[hd * V_HEAD_W:(hd + 1) * V_HEAD_W]
        vt_ref[0, hd * V_SLAB + V_HEAD_W:(hd + 1) * V_SLAB, :] = ones
    _store_rows(f_ref, (0,), slice(None), p[:, AUG_F:AUG_H])
    u_ref[0] = p[:, AUG_H:AUG_W]


def _in_proj(x, shift, scale, norm_g, lw, rope):
    b, l, d = x.shape
    tm = min(512, l)
    q_scale = math.log2(math.e) / math.sqrt(QK_NOPE_W + QK_ROPE_W)
    tok = lambda w: pl.BlockSpec((1, tm, w), lambda bi, i: (bi, i, 0))
    tok_t = lambda w: pl.BlockSpec((1, w, tm), lambda bi, i: (bi, 0, i))
    vec = pl.BlockSpec((1, 1, d), lambda bi, i: (bi, 0, 0))
    tab = pl.BlockSpec((tm, HEAD_SLAB), lambda bi, i: (i, 0))
    tab_t = pl.BlockSpec((HEAD_SLAB, tm), lambda bi, i: (0, i))
    qk_w = HEADS * HEAD_SLAB
    v_w = HEADS * V_HEAD_W
    return pl.pallas_call(
        functools.partial(_in_proj_kernel, q_scale=q_scale),
        out_shape=(jax.ShapeDtypeStruct((b, qk_w, l), MXU_DTYPE),
                   jax.ShapeDtypeStruct((b, l, qk_w), MXU_DTYPE),
                   jax.ShapeDtypeStruct((b, HEADS * V_SLAB, l), MXU_DTYPE),
                   jax.ShapeDtypeStruct((b, FNET_WIDTH // LANE_TILE, l, LANE_TILE), F32),
                   jax.ShapeDtypeStruct((b, l, 3 * HYENA_WIDTH), F32)),
        grid=(b, l // tm),
        in_specs=[tok(d), vec, vec, _const_spec((1, d)), _const_spec((d, AUG_W)),
                  _const_spec((1, Q_LORA_W)), _const_spec((1, KV_LORA_W)),
                  _const_spec((qk_w, Q_LORA_W)), _const_spec((qk_w, Q_LORA_W)),
                  _const_spec((KV_LORA_W, qk_w)), _const_spec((v_w, KV_LORA_W)), tab, tab, tab_t, tab_t],
        out_specs=(tok_t(qk_w), tok(qk_w), tok_t(HEADS * V_SLAB),
                   pl.BlockSpec((1, FNET_WIDTH // LANE_TILE, tm, LANE_TILE), lambda bi, i: (bi, 0, i, 0)),
                   tok(3 * HYENA_WIDTH)),
        compiler_params=_cparams(("parallel", "arbitrary")),
        name="in_proj",
    )(x, shift, scale, norm_g, lw["w_in"], lw["q_g"], lw["kv_g"], lw["wq_a_t"], lw["wq_b_t"],
      lw["wk"], lw["wv_t"], *rope)


def _attn_kernel(*refs, nseg, tk):
    qt_ref = refs[0]
    segs = [(refs[1 + 2 * s], refs[2 + 2 * s]) for s in range(nseg)]
    o_ref, s_ref, p_ref, ot_ref = refs[1 + 2 * nseg:]
    tq = qt_ref.shape[2]
    s_total = s_ref.shape[1]
    heads = qt_ref.shape[1] // HEAD_SLAB

    sub = 256
    chunks = []
    off = 0
    for si, (k_ref, _) in enumerate(segs):
        s_len = k_ref.shape[1]
        chunks += [(si, lo, off + lo, min(tk, s_len - lo)) for lo in range(0, s_len, tk)]
        off += s_len

    def scores_chunk(hd, chunk, mx):
        si, lo, row, n = chunk
        qt = qt_ref[0, hd * HEAD_SLAB:(hd + 1) * HEAD_SLAB, :]
        s = _dot(segs[si][0][0, lo:lo + n, hd * HEAD_SLAB:(hd + 1) * HEAD_SLAB], qt)
        s_ref[hd % 2, row:row + n, :] = s
        return jnp.maximum(mx, jnp.max(s, axis=0, keepdims=True))

    def probs_chunk(hd, chunk, mx):
        _, _, row, n = chunk
        for r in range(row, row + n, sub):
            m = min(sub, row + n - r)
            p_ref[hd % 2, r:r + m, :] = jnp.exp2(s_ref[hd % 2, r:r + m, :] - mx).astype(p_ref.dtype)

    def values_chunk(hd, chunk, acc):
        si, lo, row, n = chunk
        for r in range(0, n, sub):
            m = min(sub, n - r)
            acc = acc + _dot(segs[si][1][0, hd * V_SLAB:(hd + 1) * V_SLAB, lo + r:lo + r + m],
                             p_ref[hd % 2, row + r:row + r + m, :])
        return acc

    col_max = {}
    for step in range(heads + 2):
        mx = jnp.full((1, tq), -jnp.inf, F32)
        acc = jnp.zeros((V_SLAB, tq), F32)
        for chunk in chunks:
            if step < heads:
                mx = scores_chunk(step, chunk, mx)
            if 1 <= step <= heads:
                probs_chunk(step - 1, chunk, col_max[step - 1])
            if step >= 2:
                acc = values_chunk(step - 2, chunk, acc)
        col_max[step] = mx
        if step >= 2:
            hd = step - 2
            ot_ref[hd * V_HEAD_W:(hd + 1) * V_HEAD_W, :] = acc[:V_HEAD_W] / acc[V_HEAD_W:V_HEAD_W + 1]
    o_ref[0] = ot_ref[...].T.astype(o_ref.dtype)


def _attention(qt, kv_segs):
    b, qk_w, l = qt.shape
    tq = min(256, l)
    out_w = HEADS * V_HEAD_W
    in_specs = [pl.BlockSpec((1, qk_w, tq), lambda bi, i: (bi, 0, i))]
    args = [qt]
    for k, vt in kv_segs:
        s_len = k.shape[1]
        in_specs.append(pl.BlockSpec((1, s_len, qk_w), lambda bi, i: (bi, 0, 0)))
        in_specs.append(pl.BlockSpec((1, HEADS * V_SLAB, s_len), lambda bi, i: (bi, 0, 0)))
        args += [k, vt]
    s_total = sum(k.shape[1] for k, _ in kv_segs)
    return pl.pallas_call(
        functools.partial(_attn_kernel, nseg=len(kv_segs), tk=512),
        out_shape=jax.ShapeDtypeStruct((b, l, out_w), MXU_DTYPE),
        grid=(b, l // tq),
        in_specs=in_specs,
        out_specs=pl.BlockSpec((1, tq, out_w), lambda bi, i: (bi, i, 0)),
        scratch_shapes=[pltpu.VMEM((2, s_total, tq), F32), pltpu.VMEM((2, s_total, tq), MXU_DTYPE),
                        pltpu.VMEM((out_w, tq), F32)],
        compiler_params=_cparams(("parallel", "arbitrary")),
        name="attention",
    )(*args)


def _split(n, zero_padded):
    n1 = 1 << (int(math.log2(n)) // 2)
    n2 = n // n1
    if zero_padded and n1 // 2 < 16 and n2 > n1:
        n1, n2 = n2, n1
    return n1, n2


def _angles(rows, cols, n):
    idx = (rows[:, None] * cols[None, :]) % n
    return idx.astype(F32) * (2.0 * math.pi / n)


def _coarse_fwd_table(n1, k_in, k1_count):
    th = _angles(jnp.arange(k1_count, dtype=jnp.int32), jnp.arange(k_in, dtype=jnp.int32), n1)
    return jnp.concatenate([jnp.cos(th), -jnp.sin(th)], axis=0)


def _coarse_inv_table(n1, t_out, n, k1_count):
    k1 = jnp.arange(k1_count, dtype=jnp.int32)
    th = _angles(jnp.arange(t_out, dtype=jnp.int32), k1, n1)
    w = jnp.where((k1 == 0) | (k1 == n1 // 2), 1.0, jnp.where(k1 < n1 // 2, 2.0, 0.0)) * (1.0 / n)
    return jnp.concatenate([jnp.cos(th) * w, -jnp.sin(th) * w], axis=1)


def _fine_tables(n1, n2, k1_count):
    n = n1 * n2
    k = (jnp.arange(k1_count, dtype=jnp.int32)[:, None] + n1 * jnp.arange(n2, dtype=jnp.int32)[None, :])
    idx = (k[:, :, None] * jnp.arange(n2, dtype=jnp.int32)[None, None, :]) % n
    ph = idx.astype(F32) * (2.0 * math.pi / n)
    c, s = jnp.cos(ph), jnp.sin(ph)
    fwd = jnp.concatenate([jnp.concatenate([c, s], axis=2), jnp.concatenate([-s, c], axis=2)], axis=1)
    return fwd, jnp.swapaxes(fwd, 1, 2)


def _hermitian_planes(n1):
    return -(-(n1 // 2 + 1) // 8) * 8


PLANE_PAD_ROWS = 8


def _lmm_kernel(t_ref, x_ref, o_ref):
    o_ref[0] = _dot(t_ref[...], x_ref[0].astype(MXU_DTYPE)).astype(o_ref.dtype)


def _left_matmul(table, x, out_dtype):
    b, k, n = x.shape
    m = table.shape[0]
    tc = min(2048, n)
    return pl.pallas_call(
        _lmm_kernel,
        out_shape=jax.ShapeDtypeStruct((b, m, n), out_dtype),
        grid=(b, n // tc),
        in_specs=[_const_spec((m, k)), pl.BlockSpec((1, k, tc), lambda bi, j: (bi, 0, j))],
        out_specs=pl.BlockSpec((1, m, tc), lambda bi, j: (bi, 0, j)),
        compiler_params=_cparams(("parallel", "arbitrary")),
        name="dft_coarse",
    )(table.astype(MXU_DTYPE), x)


def _kf_mid_kernel(x_ref, g_ref, o_ref):
    n2 = x_ref.shape[3]
    c = o_ref.shape[2]
    for j in range(x_ref.shape[2]):
        zf = _dot(g_ref[j], x_ref[0, :, j].reshape(2 * n2, x_ref.shape[4]))
        o_ref[j, :n2] = (zf[:n2, :c] + zf[:n2, c:]).astype(o_ref.dtype)
        o_ref[j, n2:] = (zf[n2:, :c] - zf[n2:, c:]).astype(o_ref.dtype)


def _kf_mid(a5, g):
    _, _, k1p, n2, c2 = a5.shape
    c = c2 // 2
    t1 = 8
    return pl.pallas_call(
        _kf_mid_kernel,
        out_shape=jax.ShapeDtypeStruct((k1p, 2 * n2, c), MXU_DTYPE),
        grid=(k1p // t1,),
        in_specs=[pl.BlockSpec((1, 2, t1, n2, c2), lambda i: (0, 0, i, 0, 0)),
                  pl.BlockSpec((t1, 2 * n2, 2 * n2), lambda i: (i, 0, 0))],
        out_specs=pl.BlockSpec((t1, 2 * n2, c), lambda i: (i, 0, 0)),
        compiler_params=_cparams(("arbitrary",)),
        name="hyena_filter_spectrum",
    )(a5, g)


LANE_TILE = 128


def _split_lanes(a):
    return jnp.stack([a[..., i * LANE_TILE:(i + 1) * LANE_TILE] for i in range(a.shape[-1] // LANE_TILE)], axis=-3)


def _load_rows(ref, pre, rows):
    return jnp.concatenate([ref[pre + (t, rows, slice(None))] for t in range(ref.shape[len(pre)])], axis=1)


def _store_rows(ref, pre, rows, val):
    for t in range(ref.shape[len(pre)]):
        ref[pre + (t, rows, slice(None))] = val[:, t * LANE_TILE:(t + 1) * LANE_TILE]


def _plane_rows(plane, sp, n2):
    return pl.ds(pl.multiple_of(plane * sp, 8), n2)


def _fnet_kernel(f_ref, fa_ref, g_ref, cm_ref, o_ref, a_ref, r_ref, *, n2):
    l = f_ref.shape[2]
    n1 = l // n2
    sp = n2 + PLANE_PAD_ROWS
    fa = fa_ref[...]

    def coarse(j, carry):
        zs = _load_rows(f_ref, (0,), pl.ds(j, n1, stride=n2)).astype(MXU_DTYPE)
        _store_rows(a_ref, (), pl.ds(j, 2 * n1, stride=sp), _dot(fa, zs))
        return carry

    lax.fori_loop(0, n2, coarse, 0, unroll=4)

    def fine(k1, carry):
        x = jnp.concatenate([_load_rows(a_ref, (), _plane_rows(k1, sp, n2)),
                             _load_rows(a_ref, (), _plane_rows(n1 + k1, sp, n2))], axis=0)
        pf = _dot(g_ref[k1], x.astype(MXU_DTYPE)).astype(MXU_DTYPE)
        res = _dot(pf[:n2], cm_ref[0]) + _dot(pf[n2:], cm_ref[1])
        _store_rows(r_ref, (), pl.ds(k1, n2, stride=n1), res)
        return carry

    lax.fori_loop(0, n1, fine, 0, unroll=4)
    o_ref[0] = _load_rows(r_ref, (), slice(None)).astype(o_ref.dtype)


def _fourier_mix(f):
    b, tiles, l, _ = f.shape
    c = tiles * LANE_TILE
    n1, n2 = _split(l, False)
    fa = _coarse_fwd_table(n1, n1, n1).astype(MXU_DTYPE)
    g, _ = _fine_tables(n1, n2, n1)
    ch = jnp.arange(c, dtype=jnp.int32)
    th = _angles(ch % FNET_GROUP, ch % FNET_GROUP, FNET_GROUP)
    same = (ch[:, None] // FNET_GROUP) == (ch[None, :] // FNET_GROUP)
    norm = 1.0 / math.sqrt(l * FNET_GROUP)
    cm = jnp.stack([jnp.where(same, jnp.cos(th), 0.0), jnp.where(same, jnp.sin(th), 0.0)]) * norm
    tiles = c // LANE_TILE
    return pl.pallas_call(
        functools.partial(_fnet_kernel, n2=n2),
        out_shape=jax.ShapeDtypeStruct((b, l, c), MXU_DTYPE),
        grid=(b,),
        in_specs=[pl.BlockSpec((1, tiles, l, LANE_TILE), lambda bi: (bi, 0, 0, 0)),
                  _const_spec(fa.shape), _const_spec(g.shape), _const_spec(cm.shape)],
        out_specs=pl.BlockSpec((1, l, c), lambda bi: (bi, 0, 0)),
        scratch_shapes=[pltpu.VMEM((tiles, 2 * n1 * (n2 + PLANE_PAD_ROWS), LANE_TILE), F32),
                        pltpu.VMEM((tiles, l, LANE_TILE), F32)],
        compiler_params=_cparams(("parallel",)),
        name="fnet",
    )(f, fa, g.astype(MXU_DTYPE), cm.astype(MXU_DTYPE))


def _hy_filter_kernel(ze_ref, w1_ref, b1_ref, fr_ref, w2_ref, b2_ref, w3_ref, t_ref, dl_ref, o_ref):
    hp = lax.Precision.HIGHEST
    fr = fr_ref[...]
    h = jnp.sin(fr * (jnp.dot(ze_ref[...], w1_ref[...], precision=hp, preferred_element_type=F32) + b1_ref[...]))
    h = jnp.sin(fr * (jnp.dot(h, w2_ref[...], precision=hp, preferred_element_type=F32) + b2_ref[...]))
    h = jnp.dot(h, w3_ref[...], precision=hp, preferred_element_type=F32)
    c = dl_ref.shape[1]
    decay = jnp.exp(-t_ref[...] * dl_ref[...])
    hf = h[:, :c] * decay
    row = lax.broadcasted_iota(jnp.int32, (h.shape[0], 1), 0)
    hb = jnp.where(row == 0, 0.0, h[:, c:] * decay)
    nrm = jnp.sum(jnp.abs(hf), axis=0, keepdims=True) + jnp.sum(jnp.abs(hb), axis=0, keepdims=True)
    o_ref[:, :c] = hf / nrm
    o_ref[:, c:] = hb / nrm


def _hyena_filter_taps(l, hy):
    conv_w, conv_b, w1, b1, freq, w2, b2, w3, d_bias = hy
    emb = w1.shape[0]
    order = w1.shape[1]
    c = w3.shape[1] // 2
    t = jnp.linspace(0.0, 1.0, l, dtype=F32)[:, None]
    bands = (emb - 1) // 2
    fr = jnp.linspace(1e-4, bands - 1, bands, dtype=F32)
    ang = 2.0 * math.pi * jnp.arange(l, dtype=F32)[:, None] / l * fr
    z = jnp.concatenate([t, jnp.cos(ang), -jnp.sin(ang)], axis=-1)
    emb_pad = 128
    z = jnp.pad(z, ((0, 0), (0, emb_pad - emb)))
    w1p = jnp.pad(w1, ((0, emb_pad - emb), (0, 0)))
    min_decay = math.log(DECAY_TARGET_VAL) / SLOW_DECAY
    max_decay = math.log(DECAY_TARGET_VAL) / FAST_DECAY
    deltas = jnp.abs(jnp.linspace(min_decay, max_decay, c, dtype=F32))[None, :]
    args = (z, w1p, b1.reshape(1, order), freq.reshape(1, order), w2, b2.reshape(1, order), w3, t, deltas)
    return pl.pallas_call(
        _hy_filter_kernel,
        out_shape=jax.ShapeDtypeStruct((l, 2 * c), F32),
        grid=(1,),
        in_specs=[_const_spec(a.shape) for a in args],
        out_specs=_const_spec((l, 2 * c)),
        compiler_params=_cparams(("arbitrary",)),
        name="hyena_filter",
    )(*args)


def _hy_pre_kernel(u_ref, prev_ref, next_ref, w_ref, b_ref, x0_ref, z_ref):
    i = pl.program_id(1)
    tl = u_ref.shape[1]
    c = u_ref.shape[2] // 3
    u = u_ref[0]
    row = lax.broadcasted_iota(jnp.int32, (tl, 1), 0)
    before = jnp.where(i == 0, 0.0, prev_ref[0, 7:8, :])
    after = jnp.where(i == pl.num_programs(1) - 1, 0.0, next_ref[0, 0:1, :])
    um = jnp.where(row == 0, before, pltpu.roll(u, 1, 0))
    up = jnp.where(row == tl - 1, after, pltpu.roll(u, tl - 1, 0))
    out = b_ref[...] + um * w_ref[0:1, :] + u * w_ref[1:2, :] + up * w_ref[2:3, :]
    _store_rows(x0_ref, (0,), slice(None), out[:, :c])
    _store_rows(z_ref, (0,), slice(None), out[:, c:2 * c] * out[:, 2 * c:])


def _hyena_pre(u, conv_w, conv_b):
    b, l, w = u.shape
    c = w // 3
    tl = min(512, l)
    nb8 = tl // 8
    last8 = l // 8 - 1
    tok = lambda width: pl.BlockSpec((1, tl, width), lambda bi, i: (bi, i, 0))
    tiles = c // LANE_TILE
    split = pl.BlockSpec((1, tiles, tl, LANE_TILE), lambda bi, i: (bi, 0, i, 0))
    split_shape = jax.ShapeDtypeStruct((b, tiles, l, LANE_TILE), F32)
    return pl.pallas_call(
        _hy_pre_kernel,
        out_shape=(split_shape, split_shape),
        grid=(b, l // tl),
        in_specs=[tok(w),
                  pl.BlockSpec((1, 8, w), lambda bi, i: (bi, jnp.maximum(i * nb8 - 1, 0), 0)),
                  pl.BlockSpec((1, 8, w), lambda bi, i: (bi, jnp.minimum((i + 1) * nb8, last8), 0)),
                  _const_spec(conv_w.shape), _const_spec((1, w))],
        out_specs=(split, split),
        compiler_params=_cparams(("parallel", "arbitrary")),
        name="hyena_pre",
    )(u, u, u, conv_w, conv_b.reshape(1, w))


def _hyena_spectrum(taps):
    l, c2 = taps.shape
    n = 2 * l
    n1, n2 = _split(n, True)
    k1p = _hermitian_planes(n1)
    a = _left_matmul(_coarse_fwd_table(n1, n1 // 2, k1p), taps.reshape(1, n1 // 2, n2 * c2), MXU_DTYPE)
    g, _ = _fine_tables(n1, n2, k1p)
    return _kf_mid(a.reshape(1, 2, k1p, n2, c2), g.astype(MXU_DTYPE))


def _hyena_kernel(x0_ref, z_ref, d_ref, fa_ref, g_ref, h_ref, kf_ref, fai_ref, o_ref, a_ref, y_ref, *, n2):
    l = z_ref.shape[2]
    n1h = l // n2
    k1p = g_ref.shape[0]
    sp = n2 + PLANE_PAD_ROWS
    fa = fa_ref[...]
    fai = fai_ref[...]

    def coarse(j, carry):
        zs = _load_rows(z_ref, (0,), pl.ds(j, n1h, stride=n2)).astype(MXU_DTYPE)
        _store_rows(a_ref, (), pl.ds(j, 2 * k1p, stride=sp), _dot(fa, zs))
        return carry

    lax.fori_loop(0, n2, coarse, 0, unroll=4)

    def fine(k1, carry):
        re_rows = _plane_rows(k1, sp, n2)
        im_rows = _plane_rows(k1p + k1, sp, n2)
        x = jnp.concatenate([_load_rows(a_ref, (), re_rows), _load_rows(a_ref, (), im_rows)], axis=0)
        zf = _dot(g_ref[k1], x.astype(MXU_DTYPE))
        zr, zi = zf[:n2], zf[n2:]
        kr = kf_ref[k1, :n2].astype(F32)
        ki = kf_ref[k1, n2:].astype(F32)
        yf = jnp.concatenate([zr * kr - zi * ki, zr * ki + zi * kr], axis=0).astype(MXU_DTYPE)
        bk = _dot(h_ref[k1], yf)
        _store_rows(a_ref, (), re_rows, bk[:n2])
        _store_rows(a_ref, (), im_rows, bk[n2:])
        return carry

    lax.fori_loop(0, k1p, fine, 0, unroll=4)

    def coarse_inv(j, carry):
        bs = _load_rows(a_ref, (), pl.ds(j, 2 * k1p, stride=sp)).astype(MXU_DTYPE)
        _store_rows(y_ref, (), pl.ds(j, n1h, stride=n2), _dot(fai, bs))
        return carry

    lax.fori_loop(0, n2, coarse_inv, 0, unroll=4)
    everything = slice(None)
    gated = _load_rows(x0_ref, (0,), everything) * (
        _load_rows(y_ref, (), everything) + d_ref[...] * _load_rows(z_ref, (0,), everything))
    o_ref[0] = gated.astype(o_ref.dtype)


def _hyena_gated_conv(x0, z, d_bias, kf):
    b, tiles, l, _ = z.shape
    c = tiles * LANE_TILE
    n = 2 * l
    n1, n2 = _split(n, True)
    k1p = kf.shape[0]
    fa = _coarse_fwd_table(n1, n1 // 2, k1p).astype(MXU_DTYPE)
    fai = _coarse_inv_table(n1, n1 // 2, n, k1p).astype(MXU_DTYPE)
    g, h = _fine_tables(n1, n2, k1p)
    seq = pl.BlockSpec((1, tiles, l, LANE_TILE), lambda bi: (bi, 0, 0, 0), pipeline_mode=pl.Buffered(1))
    resident = lambda a: pl.BlockSpec(a.shape, lambda bi: (0,) * a.ndim, pipeline_mode=pl.Buffered(1))
    return pl.pallas_call(
        functools.partial(_hyena_kernel, n2=n2),
        out_shape=jax.ShapeDtypeStruct((b, l, c), MXU_DTYPE),
        grid=(b,),
        in_specs=[seq, seq, _const_spec((1, c)), _const_spec(fa.shape), resident(g), resident(h), resident(kf),
                  _const_spec(fai.shape)],
        out_specs=pl.BlockSpec((1, l, c), lambda bi: (bi, 0, 0)),
        scratch_shapes=[pltpu.VMEM((tiles, 2 * k1p * (n2 + PLANE_PAD_ROWS), LANE_TILE), F32),
                        pltpu.VMEM((tiles, l, LANE_TILE), F32)],
        compiler_params=_cparams(("parallel",)),
        name="hyena_conv",
    )(x0, z, d_bias.reshape(1, c), fa, g.astype(MXU_DTYPE), h.astype(MXU_DTYPE), kf, fai)


def _out_proj_kernel(x_ref, att_ref, f_ref, hz_ref, gate_ref, w_ref, o_ref):
    a_w = att_ref.shape[2]
    f_w = f_ref.shape[2]
    mix = (_dot(att_ref[0], w_ref[:a_w]) + _dot(f_ref[0], w_ref[a_w:a_w + f_w])
           + _dot(hz_ref[0], w_ref[a_w + f_w:]))
    o_ref[0] = x_ref[0] + gate_ref[0] * mix


def _out_proj(x, att, f, hz, gate, w_out):
    b, l, d = x.shape
    tm = min(512, l)
    tok = lambda w: pl.BlockSpec((1, tm, w), lambda bi, i: (bi, i, 0))
    return pl.pallas_call(
        _out_proj_kernel,
        out_shape=jax.ShapeDtypeStruct(x.shape, F32),
        grid=(b, l // tm),
        in_specs=[tok(d), tok(att.shape[2]), tok(f.shape[2]), tok(hz.shape[2]),
                  pl.BlockSpec((1, 1, d), lambda bi, i: (bi, 0, 0)), _const_spec(w_out.shape)],
        out_specs=tok(d),
        compiler_params=_cparams(("parallel", "arbitrary")),
        name="out_proj",
    )(x, att, f, hz, gate, w_out)


def _mlp_kernel(x_ref, sh_ref, sc_ref, gate_ref, g_ref, w1_ref, w2_ref, fg_ref, o_ref, *, chunk, final_norm):
    x = x_ref[0]
    h = (_rms(x, g_ref[...]) * (1.0 + sc_ref[0]) + sh_ref[0]).astype(MXU_DTYPE)
    acc = jnp.zeros(x.shape, F32)
    for c in range(w1_ref.shape[1] // chunk):
        a = jnp.maximum(_dot(h, w1_ref[:, c * chunk:(c + 1) * chunk]), 0.0)
        acc = acc + _dot((a * a).astype(MXU_DTYPE), w2_ref[c * chunk:(c + 1) * chunk, :])
    out = x + gate_ref[0] * acc
    if final_norm:
        out = _rms(out, fg_ref[...])
    o_ref[0] = out


def _mlp(x, shift, scale, gate, norm_g, w1, w2, final_g, final_norm):
    b, l, d = x.shape
    tm = min(512, l)
    tok = pl.BlockSpec((1, tm, d), lambda bi, i: (bi, i, 0))
    vec = pl.BlockSpec((1, 1, d), lambda bi, i: (bi, 0, 0))
    resident = lambda shape: pl.BlockSpec(shape, lambda bi, i: (0, 0), pipeline_mode=pl.Buffered(1))
    return pl.pallas_call(
        functools.partial(_mlp_kernel, chunk=512, final_norm=final_norm),
        out_shape=jax.ShapeDtypeStruct(x.shape, F32),
        grid=(b, l // tm),
        in_specs=[tok, vec, vec, vec, _const_spec((1, d)), resident(w1.shape), resident(w2.shape),
                  _const_spec((1, d))],
        out_specs=tok,
        compiler_params=_cparams(("parallel", "arbitrary")),
        name="mlp",
    )(x, shift, scale, gate, norm_g, w1, w2, final_g)


def _rope_partner_perm():
    half = AXIS_ROPE_W // 2
    partner, sign = [], []
    for j in range(QK_ROPE_W):
        first = (j % AXIS_ROPE_W) < half
        partner.append(j + half if first else j - half)
        sign.append(-1.0 if first else 1.0)
    return partner, sign


def _rope_slabs(n_tokens, rotary):
    partner, sign = _rope_partner_perm()
    if rotary:
        rows = n_tokens // ROPE_GRID_W
        row = jnp.repeat(jnp.arange(rows, dtype=F32), ROPE_GRID_W)
        col = jnp.tile(jnp.arange(ROPE_GRID_W, dtype=F32), rows)
        inv = ROPE_THETA ** (-jnp.arange(0, AXIS_ROPE_W, 2, dtype=F32) / AXIS_ROPE_W)
        ang = jnp.concatenate([row[:, None] * inv, col[:, None] * inv], axis=-1)
        cos16, sin16 = jnp.cos(ang), jnp.sin(ang)
        half = AXIS_ROPE_W // 2
        cols = jnp.array([(j // AXIS_ROPE_W) * half + j % half for j in range(QK_ROPE_W)])
        cos32 = cos16[:, cols]
        sin32 = sin16[:, cols] * jnp.array(sign, F32)
    else:
        cos32 = jnp.ones((n_tokens, QK_ROPE_W), F32)
        sin32 = jnp.zeros((n_tokens, QK_ROPE_W), F32)
    pad = HEAD_SLAB - QK_NOPE_W - QK_ROPE_W
    cos_t = jnp.concatenate([jnp.ones((n_tokens, QK_NOPE_W), F32), cos32, jnp.zeros((n_tokens, pad), F32)], axis=1)
    sin_t = jnp.concatenate([jnp.zeros((n_tokens, QK_NOPE_W), F32), sin32, jnp.zeros((n_tokens, pad), F32)], axis=1)
    return cos_t, sin_t, cos_t.T, sin_t.T


def _layer_weights(w_in, q_g, kv_g, w_uq, w_ukv):
    partner, _ = _rope_partner_perm()
    partner = jnp.array(partner)
    d = w_in.shape[0]
    pad = HEAD_SLAB - QK_NOPE_W - QK_ROPE_W
    zeros = lambda rows, w: jnp.zeros((rows, w), w_in.dtype)
    kr = w_in[:, OFF_K_ROPE:OFF_FNET]
    w_in_aug = jnp.concatenate([
        w_in[:, :OFF_K_ROPE],
        zeros(d, QK_NOPE_W), kr, zeros(d, pad),
        zeros(d, QK_NOPE_W), kr[:, partner], zeros(d, pad),
        w_in[:, OFF_FNET:]], axis=1)
    wq = w_uq.reshape(Q_LORA_W, HEADS, QK_NOPE_W + QK_ROPE_W)
    qz = jnp.zeros((Q_LORA_W, HEADS, pad), w_uq.dtype)
    wq_a = jnp.concatenate([wq, qz], axis=2).reshape(Q_LORA_W, HEADS * HEAD_SLAB)
    wq_b = jnp.concatenate([jnp.zeros((Q_LORA_W, HEADS, QK_NOPE_W), w_uq.dtype),
                            wq[:, :, QK_NOPE_W:][:, :, partner], qz], axis=2).reshape(Q_LORA_W, HEADS * HEAD_SLAB)
    wkv = w_ukv.reshape(KV_LORA_W, HEADS, QK_NOPE_W + V_HEAD_W)
    wk = jnp.concatenate([wkv[:, :, :QK_NOPE_W], jnp.zeros((KV_LORA_W, HEADS, HEAD_SLAB - QK_NOPE_W), w_ukv.dtype)],
                         axis=2).reshape(KV_LORA_W, HEADS * HEAD_SLAB)
    wv = wkv[:, :, QK_NOPE_W:].reshape(KV_LORA_W, HEADS * V_HEAD_W)
    cast = lambda a: a.astype(MXU_DTYPE)
    return {"w_in": cast(w_in_aug), "q_g": q_g.reshape(1, -1), "kv_g": kv_g.reshape(1, -1),
            "wq_a_t": cast(wq_a.T), "wq_b_t": cast(wq_b.T), "wk": cast(wk), "wv_t": cast(wv.T)}


def _heads_and_residual(x, att, f, u, hy, gate, w_out):
    conv_w, conv_b, d_bias = hy[0], hy[1], hy[8]
    fm = _fourier_mix(f)
    x0, z = _hyena_pre(u, conv_w, conv_b)
    kf = _hyena_spectrum(_hyena_filter_taps(x.shape[1], hy))
    hz = _hyena_gated_conv(x0, z, d_bias, kf)
    return _out_proj(x, att, fm, hz, gate, w_out)


def kernel(x, c, ctx, c_ctx, norm1_g, norm2_g, w_mod, b_mod, w_in, q_norm_g, kv_norm_g, w_uq, w_ukv,
           hy_conv_w, hy_conv_b, hy_w1, hy_b1, hy_freq, hy_w2, hy_b2, hy_w3, hy_d, w_out, w_mlp1,
           w_mlp2, final_norm_g):
    b, l, d = x.shape
    lc = ctx.shape[1]
    depth = w_mod.shape[0]
    rows = 16
    cc = jnp.concatenate([c, c_ctx[None, :], jnp.zeros((rows - b - 1, d), F32)], axis=0)
    mod = _modulation(cc, w_mod, b_mod)
    rope_x = _rope_slabs(l, True)
    rope_c = _rope_slabs(lc, False)
    final_g = final_norm_g.reshape(1, d)
    xc = ctx
    for li in range(depth):
        last = li == depth - 1
        mx = mod[li, :b].reshape(b, 1, N_MODULATION, d)
        mc = jnp.broadcast_to(mod[li, b].reshape(1, 1, N_MODULATION, d), (b, 1, N_MODULATION, d))
        sh1, sc1, g1, sh2, sc2, g2 = [mx[:, :, i] for i in range(N_MODULATION)]
        csh1, csc1, cg1, csh2, csc2, cg2 = [mc[:, :, i] for i in range(N_MODULATION)]
        hy = (hy_conv_w[li], hy_conv_b[li], hy_w1[li], hy_b1[li], hy_freq[li], hy_w2[li], hy_b2[li],
              hy_w3[li], hy_d[li])
        lw = _layer_weights(w_in[li], q_norm_g[li], kv_norm_g[li], w_uq[li], w_ukv[li])
        n1g = norm1_g[li].reshape(1, d)
        n2g = norm2_g[li].reshape(1, d)
        wo = w_out[li].astype(MXU_DTYPE)
        w1 = w_mlp1[li].astype(MXU_DTYPE)
        w2 = w_mlp2[li].astype(MXU_DTYPE)

        q_x, k_x, v_x, f_x, u_x = _in_proj(x, sh1, sc1, n1g, lw, rope_x)
        q_c, k_c, v_c, f_c, u_c = _in_proj(xc, csh1, csc1, n1g, lw, rope_c)
        att_x = _attention(q_x, [(k_x, v_x), (k_c, v_c)])
        x = _heads_and_residual(x, att_x, f_x, u_x, hy, g1, wo)
        x = _mlp(x, sh2, sc2, g2, n2g, w1, w2, final_g, last)
        if not last:
            att_c = _attention(q_c, [(k_c, v_c)])
            xc = _heads_and_residual(xc, att_c, f_c, u_c, hy, cg1, wo)
            xc = _mlp(xc, csh2, csc2, cg2, n2g, w1, w2, final_g, False)
    return x
```

```python
import functools
import math

import jax
import jax.numpy as jnp
from jax import lax
from jax.experimental import pallas as pl
from jax.experimental.pallas import tpu as pltpu

F32 = jnp.float32
MXU_DTYPE = jnp.bfloat16

HEADS = 8
QK_NOPE_W = 64
QK_ROPE_W = 32
V_HEAD_W = 64
Q_LORA_W = 384
KV_LORA_W = 256
AXIS_ROPE_W = QK_ROPE_W // 2
ROPE_THETA = 10000.0
ROPE_GRID_W = 64
FNET_GROUP = 64
FNET_WIDTH = 256
HYENA_WIDTH = 256
N_MODULATION = 6
NORM_EPS = 1e-6
DECAY_TARGET_VAL = 1e-2
FAST_DECAY = 0.3
SLOW_DECAY = 1.5

HEAD_SLAB = 128
V_ONES_ROWS = 16
V_SLAB = V_HEAD_W + V_ONES_ROWS
OFF_KV_LORA = Q_LORA_W
OFF_K_ROPE = OFF_KV_LORA + KV_LORA_W
OFF_FNET = OFF_K_ROPE + QK_ROPE_W
OFF_HYENA = OFF_FNET + FNET_WIDTH

AUG_KR = OFF_K_ROPE
AUG_KR_SWAP = AUG_KR + HEAD_SLAB
AUG_F = AUG_KR_SWAP + HEAD_SLAB
AUG_H = AUG_F + FNET_WIDTH
AUG_W = AUG_H + 3 * HYENA_WIDTH

V7X_VMEM_LIMIT = 56 * 1024 * 1024


def _cparams(sem):
    return pltpu.CompilerParams(dimension_semantics=sem, vmem_limit_bytes=V7X_VMEM_LIMIT)


def _dot(a, b):
    return jnp.dot(a, b, preferred_element_type=F32)


def _rms(x, g):
    return x * lax.rsqrt(jnp.mean(x * x, axis=-1, keepdims=True) + NORM_EPS) * g


def _const_spec(shape):
    n = len(shape)
    return pl.BlockSpec(shape, lambda *_: (0,) * n)


def _mod_kernel(c_ref, w_ref, b_ref, o_ref):
    c = c_ref[...]
    s = c / (1.0 + jnp.exp(-c))
    o_ref[0] = _dot(s.astype(MXU_DTYPE), w_ref[0].astype(MXU_DTYPE)) + b_ref[0]


def _modulation(cc, w_mod, b_mod):
    depth, d, n = w_mod.shape
    r = cc.shape[0]
    tn = 1024
    return pl.pallas_call(
        _mod_kernel,
        out_shape=jax.ShapeDtypeStruct((depth, r, n), F32),
        grid=(depth, n // tn),
        in_specs=[pl.BlockSpec((r, d), lambda l, j: (0, 0)),
                  pl.BlockSpec((1, d, tn), lambda l, j: (l, 0, j)),
                  pl.BlockSpec((1, 1, tn), lambda l, j: (l, 0, j))],
        out_specs=pl.BlockSpec((1, r, tn), lambda l, j: (l, 0, j)),
        compiler_params=_cparams(("arbitrary", "arbitrary")),
        name="modulation",
    )(cc, w_mod, b_mod.reshape(depth, 1, n))


def _in_proj_kernel(x_ref, sh_ref, sc_ref, g_ref, win_ref, qg_ref, kvg_ref, wqa_ref, wqb_ref,
                    wk_ref, wv_ref, cos_ref, sin_ref, cos_t_ref, sin_t_ref,
                    qt_ref, k_ref, vt_ref, f_ref, u_ref, *, q_scale):
    h = _rms(x_ref[0], g_ref[...]) * (1.0 + sc_ref[0]) + sh_ref[0]
    p = _dot(h.astype(MXU_DTYPE), win_ref[...])
    cq_t = _rms(p[:, :Q_LORA_W], qg_ref[...]).T.astype(MXU_DTYPE)
    qa_t = _dot(wqa_ref[...], cq_t)
    qb_t = _dot(wqb_ref[...], cq_t)
    rope = slice(QK_NOPE_W, QK_NOPE_W + QK_ROPE_W)
    cos_r = cos_t_ref[rope, :]
    sin_r = sin_t_ref[rope, :]
    ckv = _rms(p[:, OFF_KV_LORA:OFF_K_ROPE], kvg_ref[...])
    kn = _dot(ckv.astype(MXU_DTYPE), wk_ref[...])
    kr = p[:, AUG_KR:AUG_KR_SWAP] * cos_ref[...] + p[:, AUG_KR_SWAP:AUG_F] * sin_ref[...]
    head_w = QK_NOPE_W + QK_ROPE_W
    pad_rows = jnp.zeros((HEAD_SLAB - head_w, cq_t.shape[1]), qt_ref.dtype)
    for hd in range(HEADS):
        hs = slice(hd * HEAD_SLAB, (hd + 1) * HEAD_SLAB)
        base = hd * HEAD_SLAB
        q_nope = qa_t[hd * head_w:hd * head_w + QK_NOPE_W]
        q_rope = (qa_t[hd * head_w + QK_NOPE_W:(hd + 1) * head_w] * cos_r
                  + qb_t[hd * QK_ROPE_W:(hd + 1) * QK_ROPE_W] * sin_r)
        qt_ref[0, base:base + QK_NOPE_W, :] = (q_nope * q_scale).astype(qt_ref.dtype)
        qt_ref[0, base + QK_NOPE_W:base + head_w, :] = (q_rope * q_scale).astype(qt_ref.dtype)
        qt_ref[0, base + head_w:base + HEAD_SLAB, :] = pad_rows
        k_ref[0, :, hs] = (kn[:, hs] + kr).astype(k_ref.dtype)
    vt = _dot(wv_ref[...], ckv.T.astype(MXU_DTYPE)).astype(vt_ref.dtype)
    ones = jnp.ones((V_ONES_ROWS, vt.shape[1]), vt_ref.dtype)
    for hd in range(HEADS):
        vt_ref[0, hd * V_SLAB:hd * V_SLAB + V_HEAD_W, :] = vt[hd * V_HEAD_W:(hd + 1) * V_HEAD_W]
        vt_ref[0, hd * V_SLAB + V_HEAD_W:(hd + 1) * V_SLAB, :] = ones
    _store_rows(f_ref, (0,), slice(None), p[:, AUG_F:AUG_H])
    u_ref[0] = p[:, AUG_H:AUG_W].astype(u_ref.dtype)


def _in_proj(x, shift, scale, norm_g, lw, rope):
    b, l, d = x.shape
    tm = min(512, l)
    q_scale = math.log2(math.e) / math.sqrt(QK_NOPE_W + QK_ROPE_W)
    tok = lambda w: pl.BlockSpec((1, tm, w), lambda bi, i: (bi, i, 0))
    tok_t = lambda w: pl.BlockSpec((1, w, tm), lambda bi, i: (bi, 0, i))
    vec = pl.BlockSpec((1, 1, d), lambda bi, i: (bi, 0, 0))
    tab = pl.BlockSpec((tm, HEAD_SLAB), lambda bi, i: (i, 0))
    tab_t = pl.BlockSpec((HEAD_SLAB, tm), lambda bi, i: (0, i))
    qk_w = HEADS * HEAD_SLAB
    v_w = HEADS * V_HEAD_W
    return pl.pallas_call(
        functools.partial(_in_proj_kernel, q_scale=q_scale),
        out_shape=(jax.ShapeDtypeStruct((b, qk_w, l), MXU_DTYPE),
                   jax.ShapeDtypeStruct((b, l, qk_w), MXU_DTYPE),
                   jax.ShapeDtypeStruct((b, HEADS * V_SLAB, l), MXU_DTYPE),
                   jax.ShapeDtypeStruct((b, FNET_WIDTH // LANE_TILE, l, LANE_TILE), F32),
                   jax.ShapeDtypeStruct((b, l, 3 * HYENA_WIDTH), MXU_DTYPE)),
        grid=(b, l // tm),
        in_specs=[tok(d), vec, vec, _const_spec((1, d)), _const_spec((d, AUG_W)),
                  _const_spec((1, Q_LORA_W)), _const_spec((1, KV_LORA_W)),
                  _const_spec(lw["wq_a_t"].shape), _const_spec(lw["wq_b_t"].shape),
                  _const_spec((KV_LORA_W, qk_w)), _const_spec((v_w, KV_LORA_W)), tab, tab, tab_t, tab_t],
        out_specs=(tok_t(qk_w), tok(qk_w), tok_t(HEADS * V_SLAB),
                   pl.BlockSpec((1, FNET_WIDTH // LANE_TILE, tm, LANE_TILE), lambda bi, i: (bi, 0, i, 0)),
                   tok(3 * HYENA_WIDTH)),
        compiler_params=_cparams(("parallel", "arbitrary")),
        name="in_proj",
    )(x, shift, scale, norm_g, lw["w_in"], lw["q_g"], lw["kv_g"], lw["wq_a_t"], lw["wq_b_t"],
      lw["wk"], lw["wv_t"], *rope)


def _attn_kernel(*refs, nseg, tk):
    qt_ref = refs[0]
    segs = [(refs[1 + 2 * s], refs[2 + 2 * s]) for s in range(nseg)]
    o_ref, s_ref, p_ref, ot_ref = refs[1 + 2 * nseg:]
    tq = s_ref.shape[2]
    heads = qt_ref.shape[1] // HEAD_SLAB
    units = [(part, hd) for part in range(qt_ref.shape[2] // tq) for hd in range(heads)]

    sub = 256
    chunks = []
    off = 0
    for si, (k_ref, _) in enumerate(segs):
        s_len = k_ref.shape[1]
        chunks += [(si, lo, off + lo, min(tk, s_len - lo)) for lo in range(0, s_len, tk)]
        off += s_len

    def scores_chunk(u, chunk, mx):
        si, lo, row, n = chunk
        part, hd = units[u]
        qt = qt_ref[0, hd * HEAD_SLAB:(hd + 1) * HEAD_SLAB, part * tq:(part + 1) * tq]
        s = _dot(segs[si][0][0, lo:lo + n, hd * HEAD_SLAB:(hd + 1) * HEAD_SLAB], qt)
        s_ref[u % 2, row:row + n, :] = s
        return jnp.maximum(mx, jnp.max(s, axis=0, keepdims=True))

    def probs_chunk(u, chunk, mx):
        _, _, row, n = chunk
        for r in range(row, row + n, sub):
            m = min(sub, row + n - r)
            p_ref[u % 2, r:r + m, :] = jnp.exp2(s_ref[u % 2, r:r + m, :] - mx).astype(p_ref.dtype)

    def values_chunk(u, chunk, acc):
        si, lo, row, n = chunk
        hd = units[u][1]
        for r in range(0, n, sub):
            m = min(sub, n - r)
            acc = acc + _dot(segs[si][1][0, hd * V_SLAB:(hd + 1) * V_SLAB, lo + r:lo + r + m],
                             p_ref[u % 2, row + r:row + r + m, :])
        return acc

    col_max = {}
    for step in range(len(units) + 2):
        mx = jnp.full((1, tq), -jnp.inf, F32)
        acc = jnp.zeros((V_SLAB, tq), F32)
        for chunk in chunks:
            if step < len(units):
                mx = scores_chunk(step, chunk, mx)
            if 1 <= step <= len(units):
                probs_chunk(step - 1, chunk, col_max[step - 1])
            if step >= 2:
                acc = values_chunk(step - 2, chunk, acc)
        col_max[step] = mx
        if step >= 2:
            part, hd = units[step - 2]
            ot_ref[hd * V_HEAD_W:(hd + 1) * V_HEAD_W, part * tq:(part + 1) * tq] = (
                acc[:V_HEAD_W] / acc[V_HEAD_W:V_HEAD_W + 1])
    o_ref[0] = ot_ref[...].T.astype(o_ref.dtype)


def _attention(qt, kv_segs):
    b, qk_w, l = qt.shape
    unit_q = min(256, l)
    tq = min(2 * unit_q, l)
    out_w = HEADS * V_HEAD_W
    in_specs = [pl.BlockSpec((1, qk_w, tq), lambda bi, i: (bi, 0, i))]
    args = [qt]
    for k, vt in kv_segs:
        s_len = k.shape[1]
        in_specs.append(pl.BlockSpec((1, s_len, qk_w), lambda bi, i: (bi, 0, 0)))
        in_specs.append(pl.BlockSpec((1, HEADS * V_SLAB, s_len), lambda bi, i: (bi, 0, 0)))
        args += [k, vt]
    s_total = sum(k.shape[1] for k, _ in kv_segs)
    return pl.pallas_call(
        functools.partial(_attn_kernel, nseg=len(kv_segs), tk=512),
        out_shape=jax.ShapeDtypeStruct((b, l, out_w), MXU_DTYPE),
        grid=(b, l // tq),
        in_specs=in_specs,
        out_specs=pl.BlockSpec((1, tq, out_w), lambda bi, i: (bi, i, 0)),
        scratch_shapes=[pltpu.VMEM((2, s_total, unit_q), F32), pltpu.VMEM((2, s_total, unit_q), MXU_DTYPE),
                        pltpu.VMEM((out_w, tq), F32)],
        compiler_params=_cparams(("parallel", "arbitrary")),
        name="attention",
    )(*args)


def _split(n, zero_padded):
    n1 = 1 << (int(math.log2(n)) // 2)
    n2 = n // n1
    if zero_padded and n1 // 2 < 16 and n2 > n1:
        n1, n2 = n2, n1
    return n1, n2


def _angles(rows, cols, n):
    idx = (rows[:, None] * cols[None, :]) % n
    return idx.astype(F32) * (2.0 * math.pi / n)


def _coarse_fwd_table(n1, k_in, k1_count):
    th = _angles(jnp.arange(k1_count, dtype=jnp.int32), jnp.arange(k_in, dtype=jnp.int32), n1)
    return jnp.concatenate([jnp.cos(th), -jnp.sin(th)], axis=0)


def _coarse_inv_table(n1, t_out, n, k1_count):
    k1 = jnp.arange(k1_count, dtype=jnp.int32)
    th = _angles(jnp.arange(t_out, dtype=jnp.int32), k1, n1)
    w = jnp.where((k1 == 0) | (k1 == n1 // 2), 1.0, jnp.where(k1 < n1 // 2, 2.0, 0.0)) * (1.0 / n)
    return jnp.concatenate([jnp.cos(th) * w, -jnp.sin(th) * w], axis=1)


def _fine_tables(n1, n2, k1_count):
    n = n1 * n2
    k = (jnp.arange(k1_count, dtype=jnp.int32)[:, None] + n1 * jnp.arange(n2, dtype=jnp.int32)[None, :])
    idx = (k[:, :, None] * jnp.arange(n2, dtype=jnp.int32)[None, None, :]) % n
    ph = idx.astype(F32) * (2.0 * math.pi / n)
    c, s = jnp.cos(ph), jnp.sin(ph)
    fwd = jnp.concatenate([jnp.concatenate([c, s], axis=2), jnp.concatenate([-s, c], axis=2)], axis=1)
    return fwd, jnp.swapaxes(fwd, 1, 2)


def _hermitian_planes(n1):
    return -(-(n1 // 2 + 1) // 8) * 8


PLANE_PAD_ROWS = 8


def _lmm_kernel(t_ref, x_ref, o_ref):
    o_ref[0] = _dot(t_ref[...], x_ref[0].astype(MXU_DTYPE)).astype(o_ref.dtype)


def _left_matmul(table, x, out_dtype):
    b, k, n = x.shape
    m = table.shape[0]
    tc = min(2048, n)
    return pl.pallas_call(
        _lmm_kernel,
        out_shape=jax.ShapeDtypeStruct((b, m, n), out_dtype),
        grid=(b, n // tc),
        in_specs=[_const_spec((m, k)), pl.BlockSpec((1, k, tc), lambda bi, j: (bi, 0, j))],
        out_specs=pl.BlockSpec((1, m, tc), lambda bi, j: (bi, 0, j)),
        compiler_params=_cparams(("parallel", "arbitrary")),
        name="dft_coarse",
    )(table.astype(MXU_DTYPE), x)


def _kf_mid_kernel(x_ref, g_ref, o_ref):
    n2 = x_ref.shape[3]
    c = o_ref.shape[2]
    for j in range(x_ref.shape[2]):
        zf = _dot(g_ref[j], x_ref[0, :, j].reshape(2 * n2, x_ref.shape[4]))
        o_ref[j, :n2] = (zf[:n2, :c] + zf[:n2, c:]).astype(o_ref.dtype)
        o_ref[j, n2:] = (zf[n2:, :c] - zf[n2:, c:]).astype(o_ref.dtype)


def _kf_mid(a5, g):
    _, _, k1p, n2, c2 = a5.shape
    c = c2 // 2
    t1 = 8
    return pl.pallas_call(
        _kf_mid_kernel,
        out_shape=jax.ShapeDtypeStruct((k1p, 2 * n2, c), MXU_DTYPE),
        grid=(k1p // t1,),
        in_specs=[pl.BlockSpec((1, 2, t1, n2, c2), lambda i: (0, 0, i, 0, 0)),
                  pl.BlockSpec((t1, 2 * n2, 2 * n2), lambda i: (i, 0, 0))],
        out_specs=pl.BlockSpec((t1, 2 * n2, c), lambda i: (i, 0, 0)),
        compiler_params=_cparams(("arbitrary",)),
        name="hyena_filter_spectrum",
    )(a5, g)


LANE_TILE = 128


def _split_lanes(a):
    return jnp.stack([a[..., i * LANE_TILE:(i + 1) * LANE_TILE] for i in range(a.shape[-1] // LANE_TILE)], axis=-3)


def _load_rows(ref, pre, rows):
    return jnp.concatenate([ref[pre + (t, rows, slice(None))] for t in range(ref.shape[len(pre)])], axis=1)


def _store_rows(ref, pre, rows, val):
    for t in range(ref.shape[len(pre)]):
        ref[pre + (t, rows, slice(None))] = val[:, t * LANE_TILE:(t + 1) * LANE_TILE]


def _plane_rows(plane, sp, n2):
    return pl.ds(pl.multiple_of(plane * sp, 8), n2)


def _fnet_kernel(f_ref, fa_ref, g_ref, cm_ref, o_ref, a_ref, r_ref, *, n2):
    l = f_ref.shape[2]
    n1 = l // n2
    sp = n2 + PLANE_PAD_ROWS
    fa = fa_ref[...]

    def coarse(j, carry):
        zs = _load_rows(f_ref, (0,), pl.ds(j, n1, stride=n2)).astype(MXU_DTYPE)
        _store_rows(a_ref, (), pl.ds(j, 2 * n1, stride=sp), _dot(fa, zs))
        return carry

    lax.fori_loop(0, n2, coarse, 0, unroll=4)

    def fine(k1, carry):
        x = jnp.concatenate([_load_rows(a_ref, (), _plane_rows(k1, sp, n2)),
                             _load_rows(a_ref, (), _plane_rows(n1 + k1, sp, n2))], axis=0)
        pf = _dot(g_ref[k1], x.astype(MXU_DTYPE)).astype(MXU_DTYPE)
        res = _dot(pf[:n2], cm_ref[0]) + _dot(pf[n2:], cm_ref[1])
        _store_rows(r_ref, (), pl.ds(k1, n2, stride=n1), res)
        return carry

    lax.fori_loop(0, n1, fine, 0, unroll=4)
    o_ref[0] = _load_rows(r_ref, (), slice(None)).astype(o_ref.dtype)


def _fourier_mix(f):
    b, tiles, l, _ = f.shape
    c = tiles * LANE_TILE
    n1, n2 = _split(l, False)
    fa = _coarse_fwd_table(n1, n1, n1).astype(MXU_DTYPE)
    g, _ = _fine_tables(n1, n2, n1)
    ch = jnp.arange(c, dtype=jnp.int32)
    th = _angles(ch % FNET_GROUP, ch % FNET_GROUP, FNET_GROUP)
    same = (ch[:, None] // FNET_GROUP) == (ch[None, :] // FNET_GROUP)
    norm = 1.0 / math.sqrt(l * FNET_GROUP)
    cm = jnp.stack([jnp.where(same, jnp.cos(th), 0.0), jnp.where(same, jnp.sin(th), 0.0)]) * norm
    tiles = c // LANE_TILE
    return pl.pallas_call(
        functools.partial(_fnet_kernel, n2=n2),
        out_shape=jax.ShapeDtypeStruct((b, l, c), MXU_DTYPE),
        grid=(b,),
        in_specs=[pl.BlockSpec((1, tiles, l, LANE_TILE), lambda bi: (bi, 0, 0, 0)),
                  _const_spec(fa.shape), _const_spec(g.shape), _const_spec(cm.shape)],
        out_specs=pl.BlockSpec((1, l, c), lambda bi: (bi, 0, 0)),
        scratch_shapes=[pltpu.VMEM((tiles, 2 * n1 * (n2 + PLANE_PAD_ROWS), LANE_TILE), F32),
                        pltpu.VMEM((tiles, l, LANE_TILE), F32)],
        compiler_params=_cparams(("parallel",)),
        name="fnet",
    )(f, fa, g.astype(MXU_DTYPE), cm.astype(MXU_DTYPE))


def _hy_filter_kernel(ze_ref, w1_ref, b1_ref, fr_ref, w2_ref, b2_ref, w3_ref, t_ref, dl_ref, o_ref):
    hp = lax.Precision.HIGHEST
    fr = fr_ref[...]
    h = jnp.sin(fr * (jnp.dot(ze_ref[...], w1_ref[...], precision=hp, preferred_element_type=F32) + b1_ref[...]))
    h = jnp.sin(fr * (jnp.dot(h, w2_ref[...], precision=hp, preferred_element_type=F32) + b2_ref[...]))
    h = jnp.dot(h, w3_ref[...], precision=hp, preferred_element_type=F32)
    c = dl_ref.shape[1]
    decay = jnp.exp(-t_ref[...] * dl_ref[...])
    hf = h[:, :c] * decay
    row = lax.broadcasted_iota(jnp.int32, (h.shape[0], 1), 0)
    hb = jnp.where(row == 0, 0.0, h[:, c:] * decay)
    nrm = jnp.sum(jnp.abs(hf), axis=0, keepdims=True) + jnp.sum(jnp.abs(hb), axis=0, keepdims=True)
    o_ref[:, :c] = hf / nrm
    o_ref[:, c:] = hb / nrm


def _hyena_filter_taps(l, hy):
    conv_w, conv_b, w1, b1, freq, w2, b2, w3, d_bias = hy
    emb = w1.shape[0]
    order = w1.shape[1]
    c = w3.shape[1] // 2
    t = jnp.linspace(0.0, 1.0, l, dtype=F32)[:, None]
    bands = (emb - 1) // 2
    fr = jnp.linspace(1e-4, bands - 1, bands, dtype=F32)
    ang = 2.0 * math.pi * jnp.arange(l, dtype=F32)[:, None] / l * fr
    z = jnp.concatenate([t, jnp.cos(ang), -jnp.sin(ang)], axis=-1)
    emb_pad = 128
    z = jnp.pad(z, ((0, 0), (0, emb_pad - emb)))
    w1p = jnp.pad(w1, ((0, emb_pad - emb), (0, 0)))
    min_decay = math.log(DECAY_TARGET_VAL) / SLOW_DECAY
    max_decay = math.log(DECAY_TARGET_VAL) / FAST_DECAY
    deltas = jnp.abs(jnp.linspace(min_decay, max_decay, c, dtype=F32))[None, :]
    args = (z, w1p, b1.reshape(1, order), freq.reshape(1, order), w2, b2.reshape(1, order), w3, t, deltas)
    return pl.pallas_call(
        _hy_filter_kernel,
        out_shape=jax.ShapeDtypeStruct((l, 2 * c), F32),
        grid=(1,),
        in_specs=[_const_spec(a.shape) for a in args],
        out_specs=_const_spec((l, 2 * c)),
        compiler_params=_cparams(("arbitrary",)),
        name="hyena_filter",
    )(*args)


def _hy_pre_kernel(u_ref, prev_ref, next_ref, w_ref, b_ref, x0_ref, z_ref):
    i = pl.program_id(1)
    tl = u_ref.shape[1]
    c = u_ref.shape[2] // 3
    u = u_ref[0].astype(F32)
    row = lax.broadcasted_iota(jnp.int32, (tl, 1), 0)
    halo = prev_ref.shape[1]
    before = jnp.where(i == 0, 0.0, prev_ref[0].astype(F32)[halo - 1:halo, :])
    after = jnp.where(i == pl.num_programs(1) - 1, 0.0, next_ref[0].astype(F32)[0:1, :])
    um = jnp.where(row == 0, before, pltpu.roll(u, 1, 0))
    up = jnp.where(row == tl - 1, after, pltpu.roll(u, tl - 1, 0))
    out = b_ref[...] + um * w_ref[0:1, :] + u * w_ref[1:2, :] + up * w_ref[2:3, :]
    _store_rows(x0_ref, (0,), slice(None), out[:, :c])
    _store_rows(z_ref, (0,), slice(None), out[:, c:2 * c] * out[:, 2 * c:])


def _hyena_pre(u, conv_w, conv_b):
    b, l, w = u.shape
    c = w // 3
    tl = min(512, l)
    halo = 16
    nbh = tl // halo
    last_h = l // halo - 1
    tok = lambda width: pl.BlockSpec((1, tl, width), lambda bi, i: (bi, i, 0))
    tiles = c // LANE_TILE
    split = pl.BlockSpec((1, tiles, tl, LANE_TILE), lambda bi, i: (bi, 0, i, 0))
    split_shape = jax.ShapeDtypeStruct((b, tiles, l, LANE_TILE), F32)
    return pl.pallas_call(
        _hy_pre_kernel,
        out_shape=(split_shape, split_shape),
        grid=(b, l // tl),
        in_specs=[tok(w),
                  pl.BlockSpec((1, halo, w), lambda bi, i: (bi, jnp.maximum(i * nbh - 1, 0), 0)),
                  pl.BlockSpec((1, halo, w), lambda bi, i: (bi, jnp.minimum((i + 1) * nbh, last_h), 0)),
                  _const_spec(conv_w.shape), _const_spec((1, w))],
        out_specs=(split, split),
        compiler_params=_cparams(("parallel", "arbitrary")),
        name="hyena_pre",
    )(u, u, u, conv_w, conv_b.reshape(1, w))


def _hyena_spectrum(taps):
    l, c2 = taps.shape
    n = 2 * l
    n1, n2 = _split(n, True)
    k1p = _hermitian_planes(n1)
    a = _left_matmul(_coarse_fwd_table(n1, n1 // 2, k1p), taps.reshape(1, n1 // 2, n2 * c2), MXU_DTYPE)
    g, _ = _fine_tables(n1, n2, k1p)
    return _kf_mid(a.reshape(1, 2, k1p, n2, c2), g.astype(MXU_DTYPE))


def _hyena_kernel(x0_ref, z_ref, d_ref, fa_ref, g_ref, h_ref, kf_ref, fai_ref, o_ref, a_ref, y_ref, *, n2):
    l = z_ref.shape[2]
    n1h = l // n2
    k1p = g_ref.shape[0]
    sp = n2 + PLANE_PAD_ROWS
    fa = fa_ref[...]
    fai = fai_ref[...]

    def coarse(j, carry):
        zs = _load_rows(z_ref, (0,), pl.ds(j, n1h, stride=n2)).astype(MXU_DTYPE)
        _store_rows(a_ref, (), pl.ds(j, 2 * k1p, stride=sp), _dot(fa, zs))
        return carry

    lax.fori_loop(0, n2, coarse, 0, unroll=4)

    def fine(k1, carry):
        re_rows = _plane_rows(k1, sp, n2)
        im_rows = _plane_rows(k1p + k1, sp, n2)
        x = jnp.concatenate([_load_rows(a_ref, (), re_rows), _load_rows(a_ref, (), im_rows)], axis=0)
        zf = _dot(g_ref[k1], x.astype(MXU_DTYPE))
        zr, zi = zf[:n2], zf[n2:]
        kr = kf_ref[k1, :n2].astype(F32)
        ki = kf_ref[k1, n2:].astype(F32)
        yf = jnp.concatenate([zr * kr - zi * ki, zr * ki + zi * kr], axis=0).astype(MXU_DTYPE)
        bk = _dot(h_ref[k1], yf)
        _store_rows(a_ref, (), re_rows, bk[:n2])
        _store_rows(a_ref, (), im_rows, bk[n2:])
        return carry

    lax.fori_loop(0, k1p, fine, 0, unroll=4)

    def coarse_inv(j, carry):
        bs = _load_rows(a_ref, (), pl.ds(j, 2 * k1p, stride=sp)).astype(MXU_DTYPE)
        _store_rows(y_ref, (), pl.ds(j, n1h, stride=n2), _dot(fai, bs))
        return carry

    lax.fori_loop(0, n2, coarse_inv, 0, unroll=4)
    everything = slice(None)
    gated = _load_rows(x0_ref, (0,), everything) * (
        _load_rows(y_ref, (), everything) + d_ref[...] * _load_rows(z_ref, (0,), everything))
    o_ref[0] = gated.astype(o_ref.dtype)


def _hyena_gated_conv(x0, z, d_bias, kf):
    b, tiles, l, _ = z.shape
    c = tiles * LANE_TILE
    n = 2 * l
    n1, n2 = _split(n, True)
    k1p = kf.shape[0]
    fa = _coarse_fwd_table(n1, n1 // 2, k1p).astype(MXU_DTYPE)
    fai = _coarse_inv_table(n1, n1 // 2, n, k1p).astype(MXU_DTYPE)
    g, h = _fine_tables(n1, n2, k1p)
    seq = pl.BlockSpec((1, tiles, l, LANE_TILE), lambda bi: (bi, 0, 0, 0), pipeline_mode=pl.Buffered(1))
    resident = lambda a: pl.BlockSpec(a.shape, lambda bi: (0,) * a.ndim, pipeline_mode=pl.Buffered(1))
    return pl.pallas_call(
        functools.partial(_hyena_kernel, n2=n2),
        out_shape=jax.ShapeDtypeStruct((b, l, c), MXU_DTYPE),
        grid=(b,),
        in_specs=[seq, seq, _const_spec((1, c)), _const_spec(fa.shape), resident(g), resident(h), resident(kf),
                  _const_spec(fai.shape)],
        out_specs=pl.BlockSpec((1, l, c), lambda bi: (bi, 0, 0)),
        scratch_shapes=[pltpu.VMEM((tiles, 2 * k1p * (n2 + PLANE_PAD_ROWS), LANE_TILE), F32),
                        pltpu.VMEM((tiles, l, LANE_TILE), F32)],
        compiler_params=_cparams(("parallel",)),
        name="hyena_conv",
    )(x0, z, d_bias.reshape(1, c), fa, g.astype(MXU_DTYPE), h.astype(MXU_DTYPE), kf, fai)


def _channel_kernel(x_ref, att_ref, f_ref, hz_ref, gate1_ref, wo_ref, sh_ref, sc_ref, gate2_ref, g_ref,
                    w1_ref, w2_ref, fg_ref, o_ref, *, chunk, final_norm):
    a_w = att_ref.shape[2]
    f_w = f_ref.shape[2]
    mix = (_dot(att_ref[0], wo_ref[:a_w]) + _dot(f_ref[0], wo_ref[a_w:a_w + f_w])
           + _dot(hz_ref[0], wo_ref[a_w + f_w:]))
    x = x_ref[0] + gate1_ref[0] * mix
    h = (_rms(x, g_ref[...]) * (1.0 + sc_ref[0]) + sh_ref[0]).astype(MXU_DTYPE)
    acc = jnp.zeros(x.shape, F32)
    for c in range(w1_ref.shape[1] // chunk):
        a = jnp.maximum(_dot(h, w1_ref[:, c * chunk:(c + 1) * chunk]), 0.0)
        acc = acc + _dot((a * a).astype(MXU_DTYPE), w2_ref[c * chunk:(c + 1) * chunk, :])
    out = x + gate2_ref[0] * acc
    if final_norm:
        out = _rms(out, fg_ref[...])
    o_ref[0] = out


def _channel_mix(x, att, f, hz, gate1, w_out, shift, scale, gate2, norm_g, w1, w2, final_g, final_norm):
    b, l, d = x.shape
    tm = min(512, l)
    tok = lambda w: pl.BlockSpec((1, tm, w), lambda bi, i: (bi, i, 0))
    vec = pl.BlockSpec((1, 1, d), lambda bi, i: (bi, 0, 0))
    resident = lambda shape: pl.BlockSpec(shape, lambda bi, i: (0, 0), pipeline_mode=pl.Buffered(1))
    return pl.pallas_call(
        functools.partial(_channel_kernel, chunk=512, final_norm=final_norm),
        out_shape=jax.ShapeDtypeStruct(x.shape, F32),
        grid=(b, l // tm),
        in_specs=[tok(d), tok(att.shape[2]), tok(f.shape[2]), tok(hz.shape[2]), vec, resident(w_out.shape),
                  vec, vec, vec, _const_spec((1, d)), resident(w1.shape), resident(w2.shape),
                  _const_spec((1, d))],
        out_specs=tok(d),
        compiler_params=_cparams(("parallel", "arbitrary")),
        name="channel_mix",
    )(x, att, f, hz, gate1, w_out, shift, scale, gate2, norm_g, w1, w2, final_g)


def _rope_partner_perm():
    half = AXIS_ROPE_W // 2
    partner, sign = [], []
    for j in range(QK_ROPE_W):
        first = (j % AXIS_ROPE_W) < half
        partner.append(j + half if first else j - half)
        sign.append(-1.0 if first else 1.0)
    return partner, sign


def _rope_slabs(n_tokens, rotary):
    partner, sign = _rope_partner_perm()
    if rotary:
        rows = n_tokens // ROPE_GRID_W
        row = jnp.repeat(jnp.arange(rows, dtype=F32), ROPE_GRID_W)
        col = jnp.tile(jnp.arange(ROPE_GRID_W, dtype=F32), rows)
        inv = ROPE_THETA ** (-jnp.arange(0, AXIS_ROPE_W, 2, dtype=F32) / AXIS_ROPE_W)
        ang = jnp.concatenate([row[:, None] * inv, col[:, None] * inv], axis=-1)
        cos16, sin16 = jnp.cos(ang), jnp.sin(ang)
        half = AXIS_ROPE_W // 2
        cols = jnp.array([(j // AXIS_ROPE_W) * half + j % half for j in range(QK_ROPE_W)])
        cos32 = cos16[:, cols]
        sin32 = sin16[:, cols] * jnp.array(sign, F32)
    else:
        cos32 = jnp.ones((n_tokens, QK_ROPE_W), F32)
        sin32 = jnp.zeros((n_tokens, QK_ROPE_W), F32)
    pad = HEAD_SLAB - QK_NOPE_W - QK_ROPE_W
    cos_t = jnp.concatenate([jnp.ones((n_tokens, QK_NOPE_W), F32), cos32, jnp.zeros((n_tokens, pad), F32)], axis=1)
    sin_t = jnp.concatenate([jnp.zeros((n_tokens, QK_NOPE_W), F32), sin32, jnp.zeros((n_tokens, pad), F32)], axis=1)
    return cos_t, sin_t, cos_t.T, sin_t.T


def _layer_weights(w_in, q_g, kv_g, w_uq, w_ukv):
    partner, _ = _rope_partner_perm()
    partner = jnp.array(partner)
    d = w_in.shape[0]
    pad = HEAD_SLAB - QK_NOPE_W - QK_ROPE_W
    zeros = lambda rows, w: jnp.zeros((rows, w), w_in.dtype)
    kr = w_in[:, OFF_K_ROPE:OFF_FNET]
    w_in_aug = jnp.concatenate([
        w_in[:, :OFF_K_ROPE],
        zeros(d, QK_NOPE_W), kr, zeros(d, pad),
        zeros(d, QK_NOPE_W), kr[:, partner], zeros(d, pad),
        w_in[:, OFF_FNET:]], axis=1)
    wq = w_uq.reshape(Q_LORA_W, HEADS, QK_NOPE_W + QK_ROPE_W)
    wq_a = w_uq
    wq_b = wq[:, :, QK_NOPE_W:][:, :, partner].reshape(Q_LORA_W, HEADS * QK_ROPE_W)
    wkv = w_ukv.reshape(KV_LORA_W, HEADS, QK_NOPE_W + V_HEAD_W)
    wk = jnp.concatenate([wkv[:, :, :QK_NOPE_W], jnp.zeros((KV_LORA_W, HEADS, HEAD_SLAB - QK_NOPE_W), w_ukv.dtype)],
                         axis=2).reshape(KV_LORA_W, HEADS * HEAD_SLAB)
    wv = wkv[:, :, QK_NOPE_W:].reshape(KV_LORA_W, HEADS * V_HEAD_W)
    cast = lambda a: a.astype(MXU_DTYPE)
    return {"w_in": cast(w_in_aug), "q_g": q_g.reshape(1, -1), "kv_g": kv_g.reshape(1, -1),
            "wq_a_t": cast(wq_a.T), "wq_b_t": cast(wq_b.T), "wk": cast(wk), "wv_t": cast(wv.T)}


def _fourier_and_hyena(f, u, hy):
    conv_w, conv_b, d_bias = hy[0], hy[1], hy[8]
    fm = _fourier_mix(f)
    x0, z = _hyena_pre(u, conv_w, conv_b)
    kf = _hyena_spectrum(_hyena_filter_taps(u.shape[1], hy))
    hz = _hyena_gated_conv(x0, z, d_bias, kf)
    return fm, hz


def kernel(x, c, ctx, c_ctx, norm1_g, norm2_g, w_mod, b_mod, w_in, q_norm_g, kv_norm_g, w_uq, w_ukv,
           hy_conv_w, hy_conv_b, hy_w1, hy_b1, hy_freq, hy_w2, hy_b2, hy_w3, hy_d, w_out, w_mlp1,
           w_mlp2, final_norm_g):
    b, l, d = x.shape
    lc = ctx.shape[1]
    depth = w_mod.shape[0]
    rows = 16
    cc = jnp.concatenate([c, c_ctx[None, :], jnp.zeros((rows - b - 1, d), F32)], axis=0)
    mod = _modulation(cc, w_mod, b_mod)
    rope_x = _rope_slabs(l, True)
    rope_c = _rope_slabs(lc, False)
    final_g = final_norm_g.reshape(1, d)
    xc = ctx
    for li in range(depth):
        last = li == depth - 1
        mx = mod[li, :b].reshape(b, 1, N_MODULATION, d)
        mc = jnp.broadcast_to(mod[li, b].reshape(1, 1, N_MODULATION, d), (b, 1, N_MODULATION, d))
        sh1, sc1, g1, sh2, sc2, g2 = [mx[:, :, i] for i in range(N_MODULATION)]
        csh1, csc1, cg1, csh2, csc2, cg2 = [mc[:, :, i] for i in range(N_MODULATION)]
        hy = (hy_conv_w[li], hy_conv_b[li], hy_w1[li], hy_b1[li], hy_freq[li], hy_w2[li], hy_b2[li],
              hy_w3[li], hy_d[li])
        lw = _layer_weights(w_in[li], q_norm_g[li], kv_norm_g[li], w_uq[li], w_ukv[li])
        n1g = norm1_g[li].reshape(1, d)
        n2g = norm2_g[li].reshape(1, d)
        wo = w_out[li].astype(MXU_DTYPE)
        w1 = w_mlp1[li].astype(MXU_DTYPE)
        w2 = w_mlp2[li].astype(MXU_DTYPE)

        q_x, k_x, v_x, f_x, u_x = _in_proj(x, sh1, sc1, n1g, lw, rope_x)
        q_c, k_c, v_c, f_c, u_c = _in_proj(xc, csh1, csc1, n1g, lw, rope_c)
        att_x = _attention(q_x, [(k_x, v_x), (k_c, v_c)])
        fm_x, hz_x = _fourier_and_hyena(f_x, u_x, hy)
        x = _channel_mix(x, att_x, fm_x, hz_x, g1, wo, sh2, sc2, g2, n2g, w1, w2, final_g, last)
        if not last:
            att_c = _attention(q_c, [(k_c, v_c)])
            fm_c, hz_c = _fourier_and_hyena(f_c, u_c, hy)
            xc = _channel_mix(xc, att_c, fm_c, hz_c, cg1, wo, csh2, csc2, cg2, n2g, w1, w2, final_g, False)
    return x
```

```python
import functools
import math

import jax
import jax.numpy as jnp
from jax import lax
from jax.experimental import pallas as pl
from jax.experimental.pallas import tpu as pltpu

F32 = jnp.float32
MXU_DTYPE = jnp.bfloat16

HEADS = 8
QK_NOPE_W = 64
QK_ROPE_W = 32
V_HEAD_W = 64
Q_LORA_W = 384
KV_LORA_W = 256
AXIS_ROPE_W = QK_ROPE_W // 2
ROPE_THETA = 10000.0
ROPE_GRID_W = 64
FNET_GROUP = 64
FNET_WIDTH = 256
HYENA_WIDTH = 256
N_MODULATION = 6
NORM_EPS = 1e-6
DECAY_TARGET_VAL = 1e-2
FAST_DECAY = 0.3
SLOW_DECAY = 1.5

HEAD_SLAB = 128
V_ONES_ROWS = 16
V_SLAB = V_HEAD_W + V_ONES_ROWS
OFF_KV_LORA = Q_LORA_W
OFF_K_ROPE = OFF_KV_LORA + KV_LORA_W
OFF_FNET = OFF_K_ROPE + QK_ROPE_W
OFF_HYENA = OFF_FNET + FNET_WIDTH

AUG_KR = OFF_K_ROPE
AUG_KR_SWAP = AUG_KR + HEAD_SLAB
AUG_F = AUG_KR_SWAP + HEAD_SLAB
AUG_H = AUG_F + FNET_WIDTH
AUG_W = AUG_H + 3 * HYENA_WIDTH

V7X_VMEM_LIMIT = 56 * 1024 * 1024


def _cparams(sem):
    return pltpu.CompilerParams(dimension_semantics=sem, vmem_limit_bytes=V7X_VMEM_LIMIT)


def _dot(a, b):
    return jnp.dot(a, b, preferred_element_type=F32)


def _rms(x, g):
    return x * lax.rsqrt(jnp.mean(x * x, axis=-1, keepdims=True) + NORM_EPS) * g


def _const_spec(shape):
    n = len(shape)
    return pl.BlockSpec(shape, lambda *_: (0,) * n)


def _mod_kernel(c_ref, w_ref, b_ref, o_ref):
    c = c_ref[...]
    s = c / (1.0 + jnp.exp(-c))
    o_ref[0] = _dot(s.astype(MXU_DTYPE), w_ref[0].astype(MXU_DTYPE)) + b_ref[0]


def _modulation(cc, w_mod, b_mod):
    depth, d, n = w_mod.shape
    r = cc.shape[0]
    tn = 1024
    return pl.pallas_call(
        _mod_kernel,
        out_shape=jax.ShapeDtypeStruct((depth, r, n), F32),
        grid=(depth, n // tn),
        in_specs=[pl.BlockSpec((r, d), lambda l, j: (0, 0)),
                  pl.BlockSpec((1, d, tn), lambda l, j: (l, 0, j)),
                  pl.BlockSpec((1, 1, tn), lambda l, j: (l, 0, j))],
        out_specs=pl.BlockSpec((1, r, tn), lambda l, j: (l, 0, j)),
        compiler_params=_cparams(("arbitrary", "arbitrary")),
        name="modulation",
    )(cc, w_mod, b_mod.reshape(depth, 1, n))


def _in_proj_kernel(x_ref, sh_ref, sc_ref, g_ref, win_ref, qg_ref, kvg_ref, wqa_ref, wqb_ref,
                    wk_ref, wv_ref, cos_ref, sin_ref, cos_t_ref, sin_t_ref,
                    qt_ref, k_ref, vt_ref, f_ref, u_ref, *, q_scale):
    h = _rms(x_ref[0], g_ref[...]) * (1.0 + sc_ref[0]) + sh_ref[0]
    p = _dot(h.astype(MXU_DTYPE), win_ref[...])
    cq_t = _rms(p[:, :Q_LORA_W], qg_ref[...]).T.astype(MXU_DTYPE)
    qa_t = _dot(wqa_ref[...], cq_t)
    qb_t = _dot(wqb_ref[...], cq_t)
    rope = slice(QK_NOPE_W, QK_NOPE_W + QK_ROPE_W)
    cos_r = cos_t_ref[rope, :]
    sin_r = sin_t_ref[rope, :]
    ckv = _rms(p[:, OFF_KV_LORA:OFF_K_ROPE], kvg_ref[...])
    kn = _dot(ckv.astype(MXU_DTYPE), wk_ref[...])
    kr = p[:, AUG_KR:AUG_KR_SWAP] * cos_ref[...] + p[:, AUG_KR_SWAP:AUG_F] * sin_ref[...]
    head_w = QK_NOPE_W + QK_ROPE_W
    pad_rows = jnp.zeros((HEAD_SLAB - head_w, cq_t.shape[1]), qt_ref.dtype)
    for hd in range(HEADS):
        hs = slice(hd * HEAD_SLAB, (hd + 1) * HEAD_SLAB)
        base = hd * HEAD_SLAB
        q_nope = qa_t[hd * head_w:hd * head_w + QK_NOPE_W]
        q_rope = (qa_t[hd * head_w + QK_NOPE_W:(hd + 1) * head_w] * cos_r
                  + qb_t[hd * QK_ROPE_W:(hd + 1) * QK_ROPE_W] * sin_r)
        qt_ref[0, base:base + QK_NOPE_W, :] = (q_nope * q_scale).astype(qt_ref.dtype)
        qt_ref[0, base + QK_NOPE_W:base + head_w, :] = (q_rope * q_scale).astype(qt_ref.dtype)
        qt_ref[0, base + head_w:base + HEAD_SLAB, :] = pad_rows
        k_ref[0, :, hs] = (kn[:, hs] + kr).astype(k_ref.dtype)
    vt = _dot(wv_ref[...], ckv.T.astype(MXU_DTYPE)).astype(vt_ref.dtype)
    ones = jnp.ones((V_ONES_ROWS, vt.shape[1]), vt_ref.dtype)
    for hd in range(HEADS):
        vt_ref[0, hd * V_SLAB:hd * V_SLAB + V_HEAD_W, :] = vt[hd * V_HEAD_W:(hd + 1) * V_HEAD_W]
        vt_ref[0, hd * V_SLAB + V_HEAD_W:(hd + 1) * V_SLAB, :] = ones
    _store_rows(f_ref, (0,), slice(None), p[:, AUG_F:AUG_H])
    u_ref[0] = p[:, AUG_H:AUG_W].astype(u_ref.dtype)


def _in_proj(x, shift, scale, norm_g, lw, rope):
    b, l, d = x.shape
    tm = min(512, l)
    q_scale = math.log2(math.e) / math.sqrt(QK_NOPE_W + QK_ROPE_W)
    tok = lambda w: pl.BlockSpec((1, tm, w), lambda bi, i: (bi, i, 0))
    tok_t = lambda w: pl.BlockSpec((1, w, tm), lambda bi, i: (bi, 0, i))
    vec = pl.BlockSpec((1, 1, d), lambda bi, i: (bi, 0, 0))
    tab = pl.BlockSpec((tm, HEAD_SLAB), lambda bi, i: (i, 0))
    tab_t = pl.BlockSpec((HEAD_SLAB, tm), lambda bi, i: (0, i))
    qk_w = HEADS * HEAD_SLAB
    v_w = HEADS * V_HEAD_W
    return pl.pallas_call(
        functools.partial(_in_proj_kernel, q_scale=q_scale),
        out_shape=(jax.ShapeDtypeStruct((b, qk_w, l), MXU_DTYPE),
                   jax.ShapeDtypeStruct((b, l, qk_w), MXU_DTYPE),
                   jax.ShapeDtypeStruct((b, HEADS * V_SLAB, l), MXU_DTYPE),
                   jax.ShapeDtypeStruct((b, FNET_WIDTH // LANE_TILE, l, LANE_TILE), F32),
                   jax.ShapeDtypeStruct((b, l, 3 * HYENA_WIDTH), MXU_DTYPE)),
        grid=(b, l // tm),
        in_specs=[tok(d), vec, vec, _const_spec((1, d)), _const_spec((d, AUG_W)),
                  _const_spec((1, Q_LORA_W)), _const_spec((1, KV_LORA_W)),
                  _const_spec(lw["wq_a_t"].shape), _const_spec(lw["wq_b_t"].shape),
                  _const_spec((KV_LORA_W, qk_w)), _const_spec((v_w, KV_LORA_W)), tab, tab, tab_t, tab_t],
        out_specs=(tok_t(qk_w), tok(qk_w), tok_t(HEADS * V_SLAB),
                   pl.BlockSpec((1, FNET_WIDTH // LANE_TILE, tm, LANE_TILE), lambda bi, i: (bi, 0, i, 0)),
                   tok(3 * HYENA_WIDTH)),
        compiler_params=_cparams(("parallel", "arbitrary")),
        name="in_proj",
    )(x, shift, scale, norm_g, lw["w_in"], lw["q_g"], lw["kv_g"], lw["wq_a_t"], lw["wq_b_t"],
      lw["wk"], lw["wv_t"], *rope)


def _attn_kernel(*refs, nseg, tk):
    qt_ref = refs[0]
    segs = [(refs[1 + 2 * s], refs[2 + 2 * s]) for s in range(nseg)]
    o_ref, s_ref, p_ref, ot_ref = refs[1 + 2 * nseg:]
    tq = s_ref.shape[2]
    heads = qt_ref.shape[1] // HEAD_SLAB
    units = [(part, hd) for part in range(qt_ref.shape[2] // tq) for hd in range(heads)]

    sub = 256
    chunks = []
    off = 0
    for si, (k_ref, _) in enumerate(segs):
        s_len = k_ref.shape[1]
        chunks += [(si, lo, off + lo, min(tk, s_len - lo)) for lo in range(0, s_len, tk)]
        off += s_len

    def scores_chunk(u, chunk, mx):
        si, lo, row, n = chunk
        part, hd = units[u]
        qt = qt_ref[0, hd * HEAD_SLAB:(hd + 1) * HEAD_SLAB, part * tq:(part + 1) * tq]
        s = _dot(segs[si][0][0, lo:lo + n, hd * HEAD_SLAB:(hd + 1) * HEAD_SLAB], qt)
        s_ref[u % 2, row:row + n, :] = s
        return jnp.maximum(mx, jnp.max(s, axis=0, keepdims=True))

    def probs_chunk(u, chunk, mx):
        _, _, row, n = chunk
        for r in range(row, row + n, sub):
            m = min(sub, row + n - r)
            p_ref[u % 2, r:r + m, :] = jnp.exp2(s_ref[u % 2, r:r + m, :] - mx).astype(p_ref.dtype)

    def values_chunk(u, chunk, acc):
        si, lo, row, n = chunk
        hd = units[u][1]
        for r in range(0, n, sub):
            m = min(sub, n - r)
            acc = acc + _dot(segs[si][1][0, hd * V_SLAB:(hd + 1) * V_SLAB, lo + r:lo + r + m],
                             p_ref[u % 2, row + r:row + r + m, :])
        return acc

    col_max = {}
    for step in range(len(units) + 2):
        mx = jnp.full((1, tq), -jnp.inf, F32)
        acc = jnp.zeros((V_SLAB, tq), F32)
        for chunk in chunks:
            if step < len(units):
                mx = scores_chunk(step, chunk, mx)
            if 1 <= step <= len(units):
                probs_chunk(step - 1, chunk, col_max[step - 1])
            if step >= 2:
                acc = values_chunk(step - 2, chunk, acc)
        col_max[step] = mx
        if step >= 2:
            part, hd = units[step - 2]
            ot_ref[hd * V_HEAD_W:(hd + 1) * V_HEAD_W, part * tq:(part + 1) * tq] = (
                acc[:V_HEAD_W] / acc[V_HEAD_W:V_HEAD_W + 1])
    o_ref[0] = ot_ref[...].T.astype(o_ref.dtype)


def _attention(qt, kv_segs):
    b, qk_w, l = qt.shape
    unit_q = min(256, l)
    tq = min(2 * unit_q, l)
    out_w = HEADS * V_HEAD_W
    in_specs = [pl.BlockSpec((1, qk_w, tq), lambda bi, i: (bi, 0, i))]
    args = [qt]
    for k, vt in kv_segs:
        s_len = k.shape[1]
        in_specs.append(pl.BlockSpec((1, s_len, qk_w), lambda bi, i: (bi, 0, 0)))
        in_specs.append(pl.BlockSpec((1, HEADS * V_SLAB, s_len), lambda bi, i: (bi, 0, 0)))
        args += [k, vt]
    s_total = sum(k.shape[1] for k, _ in kv_segs)
    return pl.pallas_call(
        functools.partial(_attn_kernel, nseg=len(kv_segs), tk=512),
        out_shape=jax.ShapeDtypeStruct((b, l, out_w), MXU_DTYPE),
        grid=(b, l // tq),
        in_specs=in_specs,
        out_specs=pl.BlockSpec((1, tq, out_w), lambda bi, i: (bi, i, 0)),
        scratch_shapes=[pltpu.VMEM((2, s_total, unit_q), F32), pltpu.VMEM((2, s_total, unit_q), MXU_DTYPE),
                        pltpu.VMEM((out_w, tq), F32)],
        compiler_params=_cparams(("parallel", "arbitrary")),
        name="attention",
    )(*args)


def _split(n, zero_padded):
    n1 = 1 << (int(math.log2(n)) // 2)
    n2 = n // n1
    if zero_padded and n1 // 2 < 16 and n2 > n1:
        n1, n2 = n2, n1
    return n1, n2


def _angles(rows, cols, n):
    idx = (rows[:, None] * cols[None, :]) % n
    return idx.astype(F32) * (2.0 * math.pi / n)


def _coarse_fwd_table(n1, k_in, k1_count):
    th = _angles(jnp.arange(k1_count, dtype=jnp.int32), jnp.arange(k_in, dtype=jnp.int32), n1)
    return jnp.concatenate([jnp.cos(th), -jnp.sin(th)], axis=0)


def _coarse_inv_table(n1, t_out, n, k1_count):
    k1 = jnp.arange(k1_count, dtype=jnp.int32)
    th = _angles(jnp.arange(t_out, dtype=jnp.int32), k1, n1)
    w = jnp.where((k1 == 0) | (k1 == n1 // 2), 1.0, jnp.where(k1 < n1 // 2, 2.0, 0.0)) * (1.0 / n)
    return jnp.concatenate([jnp.cos(th) * w, -jnp.sin(th) * w], axis=1)


def _fine_tables(n1, n2, k1_count):
    n = n1 * n2
    k = (jnp.arange(k1_count, dtype=jnp.int32)[:, None] + n1 * jnp.arange(n2, dtype=jnp.int32)[None, :])
    idx = (k[:, :, None] * jnp.arange(n2, dtype=jnp.int32)[None, None, :]) % n
    ph = idx.astype(F32) * (2.0 * math.pi / n)
    c, s = jnp.cos(ph), jnp.sin(ph)
    fwd = jnp.concatenate([jnp.concatenate([c, s], axis=2), jnp.concatenate([-s, c], axis=2)], axis=1)
    return fwd, jnp.swapaxes(fwd, 1, 2)


def _hermitian_planes(n1):
    return -(-(n1 // 2 + 1) // 8) * 8


PLANE_PAD_ROWS = 8


def _lmm_kernel(t_ref, x_ref, o_ref):
    o_ref[0] = _dot(t_ref[...], x_ref[0].astype(MXU_DTYPE)).astype(o_ref.dtype)


def _left_matmul(table, x, out_dtype):
    b, k, n = x.shape
    m = table.shape[0]
    tc = min(2048, n)
    return pl.pallas_call(
        _lmm_kernel,
        out_shape=jax.ShapeDtypeStruct((b, m, n), out_dtype),
        grid=(b, n // tc),
        in_specs=[_const_spec((m, k)), pl.BlockSpec((1, k, tc), lambda bi, j: (bi, 0, j))],
        out_specs=pl.BlockSpec((1, m, tc), lambda bi, j: (bi, 0, j)),
        compiler_params=_cparams(("parallel", "arbitrary")),
        name="dft_coarse",
    )(table.astype(MXU_DTYPE), x)


def _kf_mid_kernel(x_ref, g_ref, o_ref):
    n2 = x_ref.shape[3]
    c = o_ref.shape[2]
    for j in range(x_ref.shape[2]):
        zf = _dot(g_ref[j], x_ref[0, :, j].reshape(2 * n2, x_ref.shape[4]))
        o_ref[j, :n2] = (zf[:n2, :c] + zf[:n2, c:]).astype(o_ref.dtype)
        o_ref[j, n2:] = (zf[n2:, :c] - zf[n2:, c:]).astype(o_ref.dtype)


def _kf_mid(a5, g):
    _, _, k1p, n2, c2 = a5.shape
    c = c2 // 2
    t1 = 8
    return pl.pallas_call(
        _kf_mid_kernel,
        out_shape=jax.ShapeDtypeStruct((k1p, 2 * n2, c), MXU_DTYPE),
        grid=(k1p // t1,),
        in_specs=[pl.BlockSpec((1, 2, t1, n2, c2), lambda i: (0, 0, i, 0, 0)),
                  pl.BlockSpec((t1, 2 * n2, 2 * n2), lambda i: (i, 0, 0))],
        out_specs=pl.BlockSpec((t1, 2 * n2, c), lambda i: (i, 0, 0)),
        compiler_params=_cparams(("arbitrary",)),
        name="hyena_filter_spectrum",
    )(a5, g)


LANE_TILE = 128


def _split_lanes(a):
    return jnp.stack([a[..., i * LANE_TILE:(i + 1) * LANE_TILE] for i in range(a.shape[-1] // LANE_TILE)], axis=-3)


def _load_rows(ref, pre, rows):
    return jnp.concatenate([ref[pre + (t, rows, slice(None))] for t in range(ref.shape[len(pre)])], axis=1)


def _store_rows(ref, pre, rows, val):
    for t in range(ref.shape[len(pre)]):
        ref[pre + (t, rows, slice(None))] = val[:, t * LANE_TILE:(t + 1) * LANE_TILE]


def _plane_rows(plane, sp, n2):
    return pl.ds(pl.multiple_of(plane * sp, 8), n2)


def _fnet_kernel(f_ref, fa_ref, g_ref, cm_ref, o_ref, a_ref, r_ref, *, n2):
    l = f_ref.shape[2]
    n1 = l // n2
    sp = n2 + PLANE_PAD_ROWS
    fa = fa_ref[...]

    def coarse(j, carry):
        zs = _load_rows(f_ref, (0,), pl.ds(j, n1, stride=n2)).astype(MXU_DTYPE)
        _store_rows(a_ref, (), pl.ds(j, 2 * n1, stride=sp), _dot(fa, zs))
        return carry

    lax.fori_loop(0, n2, coarse, 0, unroll=8)

    def fine(k1, carry):
        x = jnp.concatenate([_load_rows(a_ref, (), _plane_rows(k1, sp, n2)),
                             _load_rows(a_ref, (), _plane_rows(n1 + k1, sp, n2))], axis=0)
        pf = _dot(g_ref[k1], x.astype(MXU_DTYPE)).astype(MXU_DTYPE)
        res = _dot(pf[:n2], cm_ref[0]) + _dot(pf[n2:], cm_ref[1])
        _store_rows(r_ref, (), pl.ds(k1, n2, stride=n1), res)
        return carry

    lax.fori_loop(0, n1, fine, 0, unroll=8)
    o_ref[0] = _load_rows(r_ref, (), slice(None)).astype(o_ref.dtype)


def _fourier_mix(f):
    b, tiles, l, _ = f.shape
    c = tiles * LANE_TILE
    n1, n2 = _split(l, False)
    fa = _coarse_fwd_table(n1, n1, n1).astype(MXU_DTYPE)
    g, _ = _fine_tables(n1, n2, n1)
    ch = jnp.arange(c, dtype=jnp.int32)
    th = _angles(ch % FNET_GROUP, ch % FNET_GROUP, FNET_GROUP)
    same = (ch[:, None] // FNET_GROUP) == (ch[None, :] // FNET_GROUP)
    norm = 1.0 / math.sqrt(l * FNET_GROUP)
    cm = jnp.stack([jnp.where(same, jnp.cos(th), 0.0), jnp.where(same, jnp.sin(th), 0.0)]) * norm
    tiles = c // LANE_TILE
    return pl.pallas_call(
        functools.partial(_fnet_kernel, n2=n2),
        out_shape=jax.ShapeDtypeStruct((b, l, c), MXU_DTYPE),
        grid=(b,),
        in_specs=[pl.BlockSpec((1, tiles, l, LANE_TILE), lambda bi: (bi, 0, 0, 0)),
                  _const_spec(fa.shape), _const_spec(g.shape), _const_spec(cm.shape)],
        out_specs=pl.BlockSpec((1, l, c), lambda bi: (bi, 0, 0)),
        scratch_shapes=[pltpu.VMEM((tiles, 2 * n1 * (n2 + PLANE_PAD_ROWS), LANE_TILE), F32),
                        pltpu.VMEM((tiles, l, LANE_TILE), F32)],
        compiler_params=_cparams(("parallel",)),
        name="fnet",
    )(f, fa, g.astype(MXU_DTYPE), cm.astype(MXU_DTYPE))


def _hy_filter_kernel(ze_ref, w1_ref, b1_ref, fr_ref, w2_ref, b2_ref, w3_ref, t_ref, dl_ref, o_ref):
    hp = lax.Precision.HIGHEST
    fr = fr_ref[...]
    h = jnp.sin(fr * (jnp.dot(ze_ref[...], w1_ref[...], precision=hp, preferred_element_type=F32) + b1_ref[...]))
    h = jnp.sin(fr * (jnp.dot(h, w2_ref[...], precision=hp, preferred_element_type=F32) + b2_ref[...]))
    h = jnp.dot(h, w3_ref[...], precision=hp, preferred_element_type=F32)
    c = dl_ref.shape[1]
    decay = jnp.exp(-t_ref[...] * dl_ref[...])
    hf = h[:, :c] * decay
    row = lax.broadcasted_iota(jnp.int32, (h.shape[0], 1), 0)
    hb = jnp.where(row == 0, 0.0, h[:, c:] * decay)
    nrm = jnp.sum(jnp.abs(hf), axis=0, keepdims=True) + jnp.sum(jnp.abs(hb), axis=0, keepdims=True)
    o_ref[:, :c] = hf / nrm
    o_ref[:, c:] = hb / nrm


def _hyena_filter_taps(l, hy):
    conv_w, conv_b, w1, b1, freq, w2, b2, w3, d_bias = hy
    emb = w1.shape[0]
    order = w1.shape[1]
    c = w3.shape[1] // 2
    t = jnp.linspace(0.0, 1.0, l, dtype=F32)[:, None]
    bands = (emb - 1) // 2
    fr = jnp.linspace(1e-4, bands - 1, bands, dtype=F32)
    ang = 2.0 * math.pi * jnp.arange(l, dtype=F32)[:, None] / l * fr
    z = jnp.concatenate([t, jnp.cos(ang), -jnp.sin(ang)], axis=-1)
    emb_pad = 128
    z = jnp.pad(z, ((0, 0), (0, emb_pad - emb)))
    w1p = jnp.pad(w1, ((0, emb_pad - emb), (0, 0)))
    min_decay = math.log(DECAY_TARGET_VAL) / SLOW_DECAY
    max_decay = math.log(DECAY_TARGET_VAL) / FAST_DECAY
    deltas = jnp.abs(jnp.linspace(min_decay, max_decay, c, dtype=F32))[None, :]
    args = (z, w1p, b1.reshape(1, order), freq.reshape(1, order), w2, b2.reshape(1, order), w3, t, deltas)
    return pl.pallas_call(
        _hy_filter_kernel,
        out_shape=jax.ShapeDtypeStruct((l, 2 * c), F32),
        grid=(1,),
        in_specs=[_const_spec(a.shape) for a in args],
        out_specs=_const_spec((l, 2 * c)),
        compiler_params=_cparams(("arbitrary",)),
        name="hyena_filter",
    )(*args)


def _hy_pre_kernel(u_ref, prev_ref, next_ref, w_ref, b_ref, x0_ref, z_ref):
    i = pl.program_id(1)
    tl = u_ref.shape[1]
    c = u_ref.shape[2] // 3
    u = u_ref[0].astype(F32)
    row = lax.broadcasted_iota(jnp.int32, (tl, 1), 0)
    halo = prev_ref.shape[1]
    before = jnp.where(i == 0, 0.0, prev_ref[0].astype(F32)[halo - 1:halo, :])
    after = jnp.where(i == pl.num_programs(1) - 1, 0.0, next_ref[0].astype(F32)[0:1, :])
    um = jnp.where(row == 0, before, pltpu.roll(u, 1, 0))
    up = jnp.where(row == tl - 1, after, pltpu.roll(u, tl - 1, 0))
    out = b_ref[...] + um * w_ref[0:1, :] + u * w_ref[1:2, :] + up * w_ref[2:3, :]
    _store_rows(x0_ref, (0,), slice(None), out[:, :c])
    _store_rows(z_ref, (0,), slice(None), out[:, c:2 * c] * out[:, 2 * c:])


def _hyena_pre(u, conv_w, conv_b):
    b, l, w = u.shape
    c = w // 3
    tl = min(1024, l)
    halo = 16
    nbh = tl // halo
    last_h = l // halo - 1
    tok = lambda width: pl.BlockSpec((1, tl, width), lambda bi, i: (bi, i, 0))
    tiles = c // LANE_TILE
    split = pl.BlockSpec((1, tiles, tl, LANE_TILE), lambda bi, i: (bi, 0, i, 0))
    split_shape = jax.ShapeDtypeStruct((b, tiles, l, LANE_TILE), F32)
    return pl.pallas_call(
        _hy_pre_kernel,
        out_shape=(split_shape, split_shape),
        grid=(b, l // tl),
        in_specs=[tok(w),
                  pl.BlockSpec((1, halo, w), lambda bi, i: (bi, jnp.maximum(i * nbh - 1, 0), 0)),
                  pl.BlockSpec((1, halo, w), lambda bi, i: (bi, jnp.minimum((i + 1) * nbh, last_h), 0)),
                  _const_spec(conv_w.shape), _const_spec((1, w))],
        out_specs=(split, split),
        compiler_params=_cparams(("parallel", "arbitrary")),
        name="hyena_pre",
    )(u, u, u, conv_w, conv_b.reshape(1, w))


def _hyena_spectrum(taps):
    l, c2 = taps.shape
    n = 2 * l
    n1, n2 = _split(n, True)
    k1p = _hermitian_planes(n1)
    a = _left_matmul(_coarse_fwd_table(n1, n1 // 2, k1p), taps.reshape(1, n1 // 2, n2 * c2), MXU_DTYPE)
    g, _ = _fine_tables(n1, n2, k1p)
    return _kf_mid(a.reshape(1, 2, k1p, n2, c2), g.astype(MXU_DTYPE))


def _hyena_kernel(x0_ref, z_ref, d_ref, fa_ref, g_ref, h_ref, kf_ref, fai_ref, o_ref, a_ref, y_ref, *, n2):
    l = z_ref.shape[2]
    n1h = l // n2
    k1p = g_ref.shape[0]
    sp = n2 + PLANE_PAD_ROWS
    fa = fa_ref[...]
    fai = fai_ref[...]

    def coarse(j, carry):
        zs = _load_rows(z_ref, (0,), pl.ds(j, n1h, stride=n2)).astype(MXU_DTYPE)
        _store_rows(a_ref, (), pl.ds(j, 2 * k1p, stride=sp), _dot(fa, zs))
        return carry

    lax.fori_loop(0, n2, coarse, 0, unroll=8)

    def fine(k1, carry):
        re_rows = _plane_rows(k1, sp, n2)
        im_rows = _plane_rows(k1p + k1, sp, n2)
        x = jnp.concatenate([_load_rows(a_ref, (), re_rows), _load_rows(a_ref, (), im_rows)], axis=0)
        zf = _dot(g_ref[k1], x.astype(MXU_DTYPE))
        zr, zi = zf[:n2], zf[n2:]
        kr = kf_ref[k1, :n2].astype(F32)
        ki = kf_ref[k1, n2:].astype(F32)
        yf = jnp.concatenate([zr * kr - zi * ki, zr * ki + zi * kr], axis=0).astype(MXU_DTYPE)
        bk = _dot(h_ref[k1], yf)
        _store_rows(a_ref, (), re_rows, bk[:n2])
        _store_rows(a_ref, (), im_rows, bk[n2:])
        return carry

    lax.fori_loop(0, k1p, fine, 0, unroll=8)

    def coarse_inv(j, carry):
        bs = _load_rows(a_ref, (), pl.ds(j, 2 * k1p, stride=sp)).astype(MXU_DTYPE)
        _store_rows(y_ref, (), pl.ds(j, n1h, stride=n2), _dot(fai, bs))
        return carry

    lax.fori_loop(0, n2, coarse_inv, 0, unroll=8)
    everything = slice(None)
    gated = _load_rows(x0_ref, (0,), everything) * (
        _load_rows(y_ref, (), everything) + d_ref[...] * _load_rows(z_ref, (0,), everything))
    o_ref[0] = gated.astype(o_ref.dtype)


def _hyena_gated_conv(x0, z, d_bias, kf):
    b, tiles, l, _ = z.shape
    c = tiles * LANE_TILE
    n = 2 * l
    n1, n2 = _split(n, True)
    k1p = kf.shape[0]
    fa = _coarse_fwd_table(n1, n1 // 2, k1p).astype(MXU_DTYPE)
    fai = _coarse_inv_table(n1, n1 // 2, n, k1p).astype(MXU_DTYPE)
    g, h = _fine_tables(n1, n2, k1p)
    seq = pl.BlockSpec((1, tiles, l, LANE_TILE), lambda bi: (bi, 0, 0, 0), pipeline_mode=pl.Buffered(1))
    resident = lambda a: pl.BlockSpec(a.shape, lambda bi: (0,) * a.ndim, pipeline_mode=pl.Buffered(1))
    return pl.pallas_call(
        functools.partial(_hyena_kernel, n2=n2),
        out_shape=jax.ShapeDtypeStruct((b, l, c), MXU_DTYPE),
        grid=(b,),
        in_specs=[seq, seq, _const_spec((1, c)), _const_spec(fa.shape), resident(g), resident(h), resident(kf),
                  _const_spec(fai.shape)],
        out_specs=pl.BlockSpec((1, l, c), lambda bi: (bi, 0, 0)),
        scratch_shapes=[pltpu.VMEM((tiles, 2 * k1p * (n2 + PLANE_PAD_ROWS), LANE_TILE), F32),
                        pltpu.VMEM((tiles, l, LANE_TILE), F32)],
        compiler_params=_cparams(("parallel",)),
        name="hyena_conv",
    )(x0, z, d_bias.reshape(1, c), fa, g.astype(MXU_DTYPE), h.astype(MXU_DTYPE), kf, fai)


def _channel_kernel(x_ref, att_ref, f_ref, hz_ref, gate1_ref, wo_ref, sh_ref, sc_ref, gate2_ref, g_ref,
                    w1_ref, w2_ref, fg_ref, o_ref, *, chunk, final_norm):
    a_w = att_ref.shape[2]
    f_w = f_ref.shape[2]
    mix = (_dot(att_ref[0], wo_ref[:a_w]) + _dot(f_ref[0], wo_ref[a_w:a_w + f_w])
           + _dot(hz_ref[0], wo_ref[a_w + f_w:]))
    x = x_ref[0] + gate1_ref[0] * mix
    h = (_rms(x, g_ref[...]) * (1.0 + sc_ref[0]) + sh_ref[0]).astype(MXU_DTYPE)
    acc = jnp.zeros(x.shape, F32)
    for c in range(w1_ref.shape[1] // chunk):
        a = jnp.maximum(_dot(h, w1_ref[:, c * chunk:(c + 1) * chunk]), 0.0)
        acc = acc + _dot((a * a).astype(MXU_DTYPE), w2_ref[c * chunk:(c + 1) * chunk, :])
    out = x + gate2_ref[0] * acc
    if final_norm:
        out = _rms(out, fg_ref[...])
    o_ref[0] = out


def _channel_mix(x, att, f, hz, gate1, w_out, shift, scale, gate2, norm_g, w1, w2, final_g, final_norm):
    b, l, d = x.shape
    tm = min(512, l)
    tok = lambda w: pl.BlockSpec((1, tm, w), lambda bi, i: (bi, i, 0))
    vec = pl.BlockSpec((1, 1, d), lambda bi, i: (bi, 0, 0))
    resident = lambda shape: pl.BlockSpec(shape, lambda bi, i: (0, 0), pipeline_mode=pl.Buffered(1))
    return pl.pallas_call(
        functools.partial(_channel_kernel, chunk=512, final_norm=final_norm),
        out_shape=jax.ShapeDtypeStruct(x.shape, F32),
        grid=(b, l // tm),
        in_specs=[tok(d), tok(att.shape[2]), tok(f.shape[2]), tok(hz.shape[2]), vec, resident(w_out.shape),
                  vec, vec, vec, _const_spec((1, d)), resident(w1.shape), resident(w2.shape),
                  _const_spec((1, d))],
        out_specs=tok(d),
        compiler_params=_cparams(("parallel", "arbitrary")),
        name="channel_mix",
    )(x, att, f, hz, gate1, w_out, shift, scale, gate2, norm_g, w1, w2, final_g)


def _rope_partner_perm():
    half = AXIS_ROPE_W // 2
    partner, sign = [], []
    for j in range(QK_ROPE_W):
        first = (j % AXIS_ROPE_W) < half
        partner.append(j + half if first else j - half)
        sign.append(-1.0 if first else 1.0)
    return partner, sign


def _rope_slabs(n_tokens, rotary):
    partner, sign = _rope_partner_perm()
    if rotary:
        rows = n_tokens // ROPE_GRID_W
        row = jnp.repeat(jnp.arange(rows, dtype=F32), ROPE_GRID_W)
        col = jnp.tile(jnp.arange(ROPE_GRID_W, dtype=F32), rows)
        inv = ROPE_THETA ** (-jnp.arange(0, AXIS_ROPE_W, 2, dtype=F32) / AXIS_ROPE_W)
        ang = jnp.concatenate([row[:, None] * inv, col[:, None] * inv], axis=-1)
        cos16, sin16 = jnp.cos(ang), jnp.sin(ang)
        half = AXIS_ROPE_W // 2
        cols = jnp.array([(j // AXIS_ROPE_W) * half + j % half for j in range(QK_ROPE_W)])
        cos32 = cos16[:, cols]
        sin32 = sin16[:, cols] * jnp.array(sign, F32)
    else:
        cos32 = jnp.ones((n_tokens, QK_ROPE_W), F32)
        sin32 = jnp.zeros((n_tokens, QK_ROPE_W), F32)
    pad = HEAD_SLAB - QK_NOPE_W - QK_ROPE_W
    cos_t = jnp.concatenate([jnp.ones((n_tokens, QK_NOPE_W), F32), cos32, jnp.zeros((n_tokens, pad), F32)], axis=1)
    sin_t = jnp.concatenate([jnp.zeros((n_tokens, QK_NOPE_W), F32), sin32, jnp.zeros((n_tokens, pad), F32)], axis=1)
    return cos_t, sin_t, cos_t.T, sin_t.T


def _layer_weights(w_in, q_g, kv_g, w_uq, w_ukv):
    partner, _ = _rope_partner_perm()
    partner = jnp.array(partner)
    d = w_in.shape[0]
    pad = HEAD_SLAB - QK_NOPE_W - QK_ROPE_W
    zeros = lambda rows, w: jnp.zeros((rows, w), w_in.dtype)
    kr = w_in[:, OFF_K_ROPE:OFF_FNET]
    w_in_aug = jnp.concatenate([
        w_in[:, :OFF_K_ROPE],
        zeros(d, QK_NOPE_W), kr, zeros(d, pad),
        zeros(d, QK_NOPE_W), kr[:, partner], zeros(d, pad),
        w_in[:, OFF_FNET:]], axis=1)
    wq = w_uq.reshape(Q_LORA_W, HEADS, QK_NOPE_W + QK_ROPE_W)
    wq_a = w_uq
    wq_b = wq[:, :, QK_NOPE_W:][:, :, partner].reshape(Q_LORA_W, HEADS * QK_ROPE_W)
    wkv = w_ukv.reshape(KV_LORA_W, HEADS, QK_NOPE_W + V_HEAD_W)
    wk = jnp.concatenate([wkv[:, :, :QK_NOPE_W], jnp.zeros((KV_LORA_W, HEADS, HEAD_SLAB - QK_NOPE_W), w_ukv.dtype)],
                         axis=2).reshape(KV_LORA_W, HEADS * HEAD_SLAB)
    wv = wkv[:, :, QK_NOPE_W:].reshape(KV_LORA_W, HEADS * V_HEAD_W)
    cast = lambda a: a.astype(MXU_DTYPE)
    return {"w_in": cast(w_in_aug), "q_g": q_g.reshape(1, -1), "kv_g": kv_g.reshape(1, -1),
            "wq_a_t": cast(wq_a.T), "wq_b_t": cast(wq_b.T), "wk": cast(wk), "wv_t": cast(wv.T)}


def _fourier_and_hyena(f, u, hy):
    conv_w, conv_b, d_bias = hy[0], hy[1], hy[8]
    fm = _fourier_mix(f)
    x0, z = _hyena_pre(u, conv_w, conv_b)
    kf = _hyena_spectrum(_hyena_filter_taps(u.shape[1], hy))
    hz = _hyena_gated_conv(x0, z, d_bias, kf)
    return fm, hz


def kernel(x, c, ctx, c_ctx, norm1_g, norm2_g, w_mod, b_mod, w_in, q_norm_g, kv_norm_g, w_uq, w_ukv,
           hy_conv_w, hy_conv_b, hy_w1, hy_b1, hy_freq, hy_w2, hy_b2, hy_w3, hy_d, w_out, w_mlp1,
           w_mlp2, final_norm_g):
    b, l, d = x.shape
    lc = ctx.shape[1]
    depth = w_mod.shape[0]
    rows = 16
    cc = jnp.concatenate([c, c_ctx[None, :], jnp.zeros((rows - b - 1, d), F32)], axis=0)
    mod = _modulation(cc, w_mod, b_mod)
    rope_x = _rope_slabs(l, True)
    rope_c = _rope_slabs(lc, False)
    final_g = final_norm_g.reshape(1, d)
    xc = ctx
    for li in range(depth):
        last = li == depth - 1
        mx = mod[li, :b].reshape(b, 1, N_MODULATION, d)
        mc = jnp.broadcast_to(mod[li, b].reshape(1, 1, N_MODULATION, d), (b, 1, N_MODULATION, d))
        sh1, sc1, g1, sh2, sc2, g2 = [mx[:, :, i] for i in range(N_MODULATION)]
        csh1, csc1, cg1, csh2, csc2, cg2 = [mc[:, :, i] for i in range(N_MODULATION)]
        hy = (hy_conv_w[li], hy_conv_b[li], hy_w1[li], hy_b1[li], hy_freq[li], hy_w2[li], hy_b2[li],
              hy_w3[li], hy_d[li])
        lw = _layer_weights(w_in[li], q_norm_g[li], kv_norm_g[li], w_uq[li], w_ukv[li])
        n1g = norm1_g[li].reshape(1, d)
        n2g = norm2_g[li].reshape(1, d)
        wo = w_out[li].astype(MXU_DTYPE)
        w1 = w_mlp1[li].astype(MXU_DTYPE)
        w2 = w_mlp2[li].astype(MXU_DTYPE)

        q_x, k_x, v_x, f_x, u_x = _in_proj(x, sh1, sc1, n1g, lw, rope_x)
        q_c, k_c, v_c, f_c, u_c = _in_proj(xc, csh1, csc1, n1g, lw, rope_c)
        att_x = _attention(q_x, [(k_x, v_x), (k_c, v_c)])
        fm_x, hz_x = _fourier_and_hyena(f_x, u_x, hy)
        x = _channel_mix(x, att_x, fm_x, hz_x, g1, wo, sh2, sc2, g2, n2g, w1, w2, final_g, last)
        if not last:
            att_c = _attention(q_c, [(k_c, v_c)])
            fm_c, hz_c = _fourier_and_hyena(f_c, u_c, hy)
            xc = _channel_mix(xc, att_c, fm_c, hz_c, cg1, wo, csh2, csc2, cg2, n2g, w1, w2, final_g, False)
    return x
```

```python
import functools
import math

import jax
import jax.numpy as jnp
from jax import lax
from jax.experimental import pallas as pl
from jax.experimental.pallas import tpu as pltpu

F32 = jnp.float32
MXU_DTYPE = jnp.bfloat16

HEADS = 8
QK_NOPE_W = 64
QK_ROPE_W = 32
V_HEAD_W = 64
Q_LORA_W = 384
KV_LORA_W = 256
AXIS_ROPE_W = QK_ROPE_W // 2
ROPE_THETA = 10000.0
ROPE_GRID_W = 64
FNET_GROUP = 64
FNET_WIDTH = 256
HYENA_WIDTH = 256
N_MODULATION = 6
NORM_EPS = 1e-6
DECAY_TARGET_VAL = 1e-2
FAST_DECAY = 0.3
SLOW_DECAY = 1.5

HEAD_SLAB = 128
V_ONES_ROWS = 16
V_SLAB = V_HEAD_W + V_ONES_ROWS
OFF_KV_LORA = Q_LORA_W
OFF_K_ROPE = OFF_KV_LORA + KV_LORA_W
OFF_FNET = OFF_K_ROPE + QK_ROPE_W
OFF_HYENA = OFF_FNET + FNET_WIDTH

AUG_KR = OFF_K_ROPE
AUG_KR_SWAP = AUG_KR + HEAD_SLAB
AUG_F = AUG_KR_SWAP + HEAD_SLAB
AUG_H = AUG_F + FNET_WIDTH
AUG_W = AUG_H + 3 * HYENA_WIDTH

V7X_VMEM_LIMIT = 56 * 1024 * 1024


def _cparams(sem):
    return pltpu.CompilerParams(dimension_semantics=sem, vmem_limit_bytes=V7X_VMEM_LIMIT)


def _dot(a, b):
    return jnp.dot(a, b, preferred_element_type=F32)


def _dot_split(a, b):
    a_hi = a.astype(MXU_DTYPE)
    b_hi = b.astype(MXU_DTYPE)
    a_lo = (a - a_hi.astype(F32)).astype(MXU_DTYPE)
    b_lo = (b - b_hi.astype(F32)).astype(MXU_DTYPE)
    return _dot(a_hi, b_hi) + (_dot(a_hi, b_lo) + _dot(a_lo, b_hi))


def _rms(x, g):
    return x * lax.rsqrt(jnp.mean(x * x, axis=-1, keepdims=True) + NORM_EPS) * g


def _const_spec(shape):
    n = len(shape)
    return pl.BlockSpec(shape, lambda *_: (0,) * n)


def _mod_kernel(c_ref, w_ref, b_ref, o_ref):
    c = c_ref[...]
    s = c / (1.0 + jnp.exp(-c))
    o_ref[0] = _dot(s.astype(MXU_DTYPE), w_ref[0].astype(MXU_DTYPE)) + b_ref[0]


def _modulation(cc, w_mod, b_mod):
    depth, d, n = w_mod.shape
    r = cc.shape[0]
    tn = 1024
    return pl.pallas_call(
        _mod_kernel,
        out_shape=jax.ShapeDtypeStruct((depth, r, n), F32),
        grid=(depth, n // tn),
        in_specs=[pl.BlockSpec((r, d), lambda l, j: (0, 0)),
                  pl.BlockSpec((1, d, tn), lambda l, j: (l, 0, j)),
                  pl.BlockSpec((1, 1, tn), lambda l, j: (l, 0, j))],
        out_specs=pl.BlockSpec((1, r, tn), lambda l, j: (l, 0, j)),
        compiler_params=_cparams(("arbitrary", "arbitrary")),
        name="modulation",
    )(cc, w_mod, b_mod.reshape(depth, 1, n))


def _in_proj_kernel(x_ref, sh_ref, sc_ref, g_ref, win_ref, qg_ref, kvg_ref, wqa_ref, wqb_ref,
                    wk_ref, wv_ref, cos_ref, sin_ref, cos_t_ref, sin_t_ref,
                    qt_ref, k_ref, vt_ref, f_ref, u_ref, *, q_scale):
    h = _rms(x_ref[0], g_ref[...]) * (1.0 + sc_ref[0]) + sh_ref[0]
    p = _dot(h.astype(MXU_DTYPE), win_ref[...])
    cq_t = _rms(p[:, :Q_LORA_W], qg_ref[...]).T.astype(MXU_DTYPE)
    qa_t = _dot(wqa_ref[...], cq_t)
    qb_t = _dot(wqb_ref[...], cq_t)
    rope = slice(QK_NOPE_W, QK_NOPE_W + QK_ROPE_W)
    cos_r = cos_t_ref[rope, :]
    sin_r = sin_t_ref[rope, :]
    ckv = _rms(p[:, OFF_KV_LORA:OFF_K_ROPE], kvg_ref[...])
    kn = _dot(ckv.astype(MXU_DTYPE), wk_ref[...])
    kr = p[:, AUG_KR:AUG_KR_SWAP] * cos_ref[...] + p[:, AUG_KR_SWAP:AUG_F] * sin_ref[...]
    head_w = QK_NOPE_W + QK_ROPE_W
    pad_rows = jnp.zeros((HEAD_SLAB - head_w, cq_t.shape[1]), qt_ref.dtype)
    for hd in range(HEADS):
        hs = slice(hd * HEAD_SLAB, (hd + 1) * HEAD_SLAB)
        base = hd * HEAD_SLAB
        q_nope = qa_t[hd * head_w:hd * head_w + QK_NOPE_W]
        q_rope = (qa_t[hd * head_w + QK_NOPE_W:(hd + 1) * head_w] * cos_r
                  + qb_t[hd * QK_ROPE_W:(hd + 1) * QK_ROPE_W] * sin_r)
        qt_ref[0, base:base + QK_NOPE_W, :] = (q_nope * q_scale).astype(qt_ref.dtype)
        qt_ref[0, base + QK_NOPE_W:base + head_w, :] = (q_rope * q_scale).astype(qt_ref.dtype)
        qt_ref[0, base + head_w:base + HEAD_SLAB, :] = pad_rows
        k_ref[0, :, hs] = (kn[:, hs] + kr).astype(k_ref.dtype)
    vt = _dot(wv_ref[...], ckv.T.astype(MXU_DTYPE)).astype(vt_ref.dtype)
    ones = jnp.ones((V_ONES_ROWS, vt.shape[1]), vt_ref.dtype)
    for hd in range(HEADS):
        vt_ref[0, hd * V_SLAB:hd * V_SLAB + V_HEAD_W, :] = vt[hd * V_HEAD_W:(hd + 1) * V_HEAD_W]
        vt_ref[0, hd * V_SLAB + V_HEAD_W:(hd + 1) * V_SLAB, :] = ones
    _store_rows(f_ref, (0,), slice(None), p[:, AUG_F:AUG_H])
    u_ref[0] = p[:, AUG_H:AUG_W].astype(u_ref.dtype)


def _in_proj(x, shift, scale, norm_g, lw, rope):
    b, l, d = x.shape
    tm = min(512, l)
    q_scale = math.log2(math.e) / math.sqrt(QK_NOPE_W + QK_ROPE_W)
    tok = lambda w: pl.BlockSpec((1, tm, w), lambda bi, i: (bi, i, 0))
    tok_t = lambda w: pl.BlockSpec((1, w, tm), lambda bi, i: (bi, 0, i))
    vec = pl.BlockSpec((1, 1, d), lambda bi, i: (bi, 0, 0))
    tab = pl.BlockSpec((tm, HEAD_SLAB), lambda bi, i: (i, 0))
    tab_t = pl.BlockSpec((HEAD_SLAB, tm), lambda bi, i: (0, i))
    qk_w = HEADS * HEAD_SLAB
    v_w = HEADS * V_HEAD_W
    return pl.pallas_call(
        functools.partial(_in_proj_kernel, q_scale=q_scale),
        out_shape=(jax.ShapeDtypeStruct((b, qk_w, l), MXU_DTYPE),
                   jax.ShapeDtypeStruct((b, l, qk_w), MXU_DTYPE),
                   jax.ShapeDtypeStruct((b, HEADS * V_SLAB, l), MXU_DTYPE),
                   jax.ShapeDtypeStruct((b, FNET_WIDTH // LANE_TILE, l, LANE_TILE), F32),
                   jax.ShapeDtypeStruct((b, l, 3 * HYENA_WIDTH), MXU_DTYPE)),
        grid=(b, l // tm),
        in_specs=[tok(d), vec, vec, _const_spec((1, d)), _const_spec((d, AUG_W)),
                  _const_spec((1, Q_LORA_W)), _const_spec((1, KV_LORA_W)),
                  _const_spec(lw["wq_a_t"].shape), _const_spec(lw["wq_b_t"].shape),
                  _const_spec((KV_LORA_W, qk_w)), _const_spec((v_w, KV_LORA_W)), tab, tab, tab_t, tab_t],
        out_specs=(tok_t(qk_w), tok(qk_w), tok_t(HEADS * V_SLAB),
                   pl.BlockSpec((1, FNET_WIDTH // LANE_TILE, tm, LANE_TILE), lambda bi, i: (bi, 0, i, 0)),
                   tok(3 * HYENA_WIDTH)),
        compiler_params=_cparams(("parallel", "arbitrary")),
        name="in_proj",
    )(x, shift, scale, norm_g, lw["w_in"], lw["q_g"], lw["kv_g"], lw["wq_a_t"], lw["wq_b_t"],
      lw["wk"], lw["wv_t"], *rope)


def _attn_kernel(*refs, nseg, tk):
    qt_ref = refs[0]
    segs = [(refs[1 + 2 * s], refs[2 + 2 * s]) for s in range(nseg)]
    o_ref, s_ref, p_ref, ot_ref = refs[1 + 2 * nseg:]
    tq = s_ref.shape[2]
    heads = qt_ref.shape[1] // HEAD_SLAB
    units = [(part, hd) for part in range(qt_ref.shape[2] // tq) for hd in range(heads)]

    sub = 256
    chunks = []
    off = 0
    for si, (k_ref, _) in enumerate(segs):
        s_len = k_ref.shape[1]
        chunks += [(si, lo, off + lo, min(tk, s_len - lo)) for lo in range(0, s_len, tk)]
        off += s_len

    def scores_chunk(u, chunk, mx):
        si, lo, row, n = chunk
        part, hd = units[u]
        qt = qt_ref[0, hd * HEAD_SLAB:(hd + 1) * HEAD_SLAB, part * tq:(part + 1) * tq]
        s = _dot(segs[si][0][0, lo:lo + n, hd * HEAD_SLAB:(hd + 1) * HEAD_SLAB], qt)
        s_ref[u % 2, row:row + n, :] = s
        return jnp.maximum(mx, jnp.max(s, axis=0, keepdims=True))

    def probs_chunk(u, chunk, mx):
        _, _, row, n = chunk
        for r in range(row, row + n, sub):
            m = min(sub, row + n - r)
            p_ref[u % 2, r:r + m, :] = jnp.exp2(s_ref[u % 2, r:r + m, :] - mx).astype(p_ref.dtype)

    def values_chunk(u, chunk, acc):
        si, lo, row, n = chunk
        hd = units[u][1]
        for r in range(0, n, sub):
            m = min(sub, n - r)
            acc = acc + _dot(segs[si][1][0, hd * V_SLAB:(hd + 1) * V_SLAB, lo + r:lo + r + m],
                             p_ref[u % 2, row + r:row + r + m, :])
        return acc

    col_max = {}
    for step in range(len(units) + 2):
        mx = jnp.full((1, tq), -jnp.inf, F32)
        acc = jnp.zeros((V_SLAB, tq), F32)
        for chunk in chunks:
            if step < len(units):
                mx = scores_chunk(step, chunk, mx)
            if 1 <= step <= len(units):
                probs_chunk(step - 1, chunk, col_max[step - 1])
            if step >= 2:
                acc = values_chunk(step - 2, chunk, acc)
        col_max[step] = mx
        if step >= 2:
            part, hd = units[step - 2]
            ot_ref[hd * V_HEAD_W:(hd + 1) * V_HEAD_W, part * tq:(part + 1) * tq] = (
                acc[:V_HEAD_W] / acc[V_HEAD_W:V_HEAD_W + 1])
    o_ref[0] = ot_ref[...].T.astype(o_ref.dtype)


def _attention(qt, kv_segs):
    b, qk_w, l = qt.shape
    unit_q = min(256, l)
    tq = min(2 * unit_q, l)
    out_w = HEADS * V_HEAD_W
    in_specs = [pl.BlockSpec((1, qk_w, tq), lambda bi, i: (bi, 0, i))]
    args = [qt]
    for k, vt in kv_segs:
        s_len = k.shape[1]
        in_specs.append(pl.BlockSpec((1, s_len, qk_w), lambda bi, i: (bi, 0, 0)))
        in_specs.append(pl.BlockSpec((1, HEADS * V_SLAB, s_len), lambda bi, i: (bi, 0, 0)))
        args += [k, vt]
    s_total = sum(k.shape[1] for k, _ in kv_segs)
    return pl.pallas_call(
        functools.partial(_attn_kernel, nseg=len(kv_segs), tk=512),
        out_shape=jax.ShapeDtypeStruct((b, l, out_w), MXU_DTYPE),
        grid=(b, l // tq),
        in_specs=in_specs,
        out_specs=pl.BlockSpec((1, tq, out_w), lambda bi, i: (bi, i, 0)),
        scratch_shapes=[pltpu.VMEM((2, s_total, unit_q), F32), pltpu.VMEM((2, s_total, unit_q), MXU_DTYPE),
                        pltpu.VMEM((out_w, tq), F32)],
        compiler_params=_cparams(("parallel", "arbitrary")),
        name="attention",
    )(*args)


def _split(n, zero_padded):
    n1 = 1 << (int(math.log2(n)) // 2)
    n2 = n // n1
    if zero_padded and n1 // 2 < 16 and n2 > n1:
        n1, n2 = n2, n1
    return n1, n2


def _angles(rows, cols, n):
    idx = (rows[:, None] * cols[None, :]) % n
    return idx.astype(F32) * (2.0 * math.pi / n)


def _coarse_fwd_table(n1, k_in, k1_count):
    th = _angles(jnp.arange(k1_count, dtype=jnp.int32), jnp.arange(k_in, dtype=jnp.int32), n1)
    return jnp.concatenate([jnp.cos(th), -jnp.sin(th)], axis=0)


def _coarse_inv_table(n1, t_out, n, k1_count):
    k1 = jnp.arange(k1_count, dtype=jnp.int32)
    th = _angles(jnp.arange(t_out, dtype=jnp.int32), k1, n1)
    w = jnp.where((k1 == 0) | (k1 == n1 // 2), 1.0, jnp.where(k1 < n1 // 2, 2.0, 0.0)) * (1.0 / n)
    return jnp.concatenate([jnp.cos(th) * w, -jnp.sin(th) * w], axis=1)


def _fine_tables(n1, n2, k1_count):
    n = n1 * n2
    k = (jnp.arange(k1_count, dtype=jnp.int32)[:, None] + n1 * jnp.arange(n2, dtype=jnp.int32)[None, :])
    idx = (k[:, :, None] * jnp.arange(n2, dtype=jnp.int32)[None, None, :]) % n
    ph = idx.astype(F32) * (2.0 * math.pi / n)
    c, s = jnp.cos(ph), jnp.sin(ph)
    fwd = jnp.concatenate([jnp.concatenate([c, s], axis=2), jnp.concatenate([-s, c], axis=2)], axis=1)
    return fwd, jnp.swapaxes(fwd, 1, 2)


def _hermitian_planes(n1):
    return -(-(n1 // 2 + 1) // 8) * 8


PLANE_PAD_ROWS = 8


def _lmm_kernel(t_ref, x_ref, o_ref):
    o_ref[0] = _dot(t_ref[...], x_ref[0].astype(MXU_DTYPE)).astype(o_ref.dtype)


def _left_matmul(table, x, out_dtype):
    b, k, n = x.shape
    m = table.shape[0]
    tc = min(2048, n)
    return pl.pallas_call(
        _lmm_kernel,
        out_shape=jax.ShapeDtypeStruct((b, m, n), out_dtype),
        grid=(b, n // tc),
        in_specs=[_const_spec((m, k)), pl.BlockSpec((1, k, tc), lambda bi, j: (bi, 0, j))],
        out_specs=pl.BlockSpec((1, m, tc), lambda bi, j: (bi, 0, j)),
        compiler_params=_cparams(("parallel", "arbitrary")),
        name="dft_coarse",
    )(table.astype(MXU_DTYPE), x)


def _kf_mid_kernel(x_ref, g_ref, o_ref):
    n2 = x_ref.shape[3]
    c = o_ref.shape[2]
    for j in range(x_ref.shape[2]):
        zf = _dot(g_ref[j], x_ref[0, :, j].reshape(2 * n2, x_ref.shape[4]))
        o_ref[j, :n2] = (zf[:n2, :c] + zf[:n2, c:]).astype(o_ref.dtype)
        o_ref[j, n2:] = (zf[n2:, :c] - zf[n2:, c:]).astype(o_ref.dtype)


def _kf_mid(a5, g):
    _, _, k1p, n2, c2 = a5.shape
    c = c2 // 2
    t1 = 8
    return pl.pallas_call(
        _kf_mid_kernel,
        out_shape=jax.ShapeDtypeStruct((k1p, 2 * n2, c), MXU_DTYPE),
        grid=(k1p // t1,),
        in_specs=[pl.BlockSpec((1, 2, t1, n2, c2), lambda i: (0, 0, i, 0, 0)),
                  pl.BlockSpec((t1, 2 * n2, 2 * n2), lambda i: (i, 0, 0))],
        out_specs=pl.BlockSpec((t1, 2 * n2, c), lambda i: (i, 0, 0)),
        compiler_params=_cparams(("arbitrary",)),
        name="hyena_filter_spectrum",
    )(a5, g)


LANE_TILE = 128


def _split_lanes(a):
    return jnp.stack([a[..., i * LANE_TILE:(i + 1) * LANE_TILE] for i in range(a.shape[-1] // LANE_TILE)], axis=-3)


def _load_rows(ref, pre, rows):
    return jnp.concatenate([ref[pre + (t, rows, slice(None))] for t in range(ref.shape[len(pre)])], axis=1)


def _store_rows(ref, pre, rows, val):
    for t in range(ref.shape[len(pre)]):
        ref[pre + (t, rows, slice(None))] = val[:, t * LANE_TILE:(t + 1) * LANE_TILE]


def _plane_rows(plane, sp, n2):
    return pl.ds(pl.multiple_of(plane * sp, 8), n2)


def _fnet_kernel(f_ref, fa_ref, g_ref, cm_ref, o_ref, a_ref, r_ref, *, n2):
    l = f_ref.shape[2]
    n1 = l // n2
    sp = n2 + PLANE_PAD_ROWS
    fa = fa_ref[...]

    def coarse(j, carry):
        zs = _load_rows(f_ref, (0,), pl.ds(j, n1, stride=n2)).astype(MXU_DTYPE)
        _store_rows(a_ref, (), pl.ds(j, 2 * n1, stride=sp), _dot(fa, zs))
        return carry

    lax.fori_loop(0, n2, coarse, 0, unroll=16)

    def fine(k1, carry):
        x = jnp.concatenate([_load_rows(a_ref, (), _plane_rows(k1, sp, n2)),
                             _load_rows(a_ref, (), _plane_rows(n1 + k1, sp, n2))], axis=0)
        pf = _dot(g_ref[k1], x.astype(MXU_DTYPE)).astype(MXU_DTYPE)
        res = _dot(jnp.concatenate([pf[:n2], pf[n2:]], axis=1), cm_ref[...])
        _store_rows(r_ref, (), pl.ds(k1, n2, stride=n1), res)
        return carry

    lax.fori_loop(0, n1, fine, 0, unroll=8)
    o_ref[0] = _load_rows(r_ref, (), slice(None)).astype(o_ref.dtype)


def _fourier_mix(f):
    b, tiles, l, _ = f.shape
    c = tiles * LANE_TILE
    n1, n2 = _split(l, False)
    fa = _coarse_fwd_table(n1, n1, n1).astype(MXU_DTYPE)
    g, _ = _fine_tables(n1, n2, n1)
    ch = jnp.arange(c, dtype=jnp.int32)
    th = _angles(ch % FNET_GROUP, ch % FNET_GROUP, FNET_GROUP)
    same = (ch[:, None] // FNET_GROUP) == (ch[None, :] // FNET_GROUP)
    norm = 1.0 / math.sqrt(l * FNET_GROUP)
    cm = jnp.concatenate([jnp.where(same, jnp.cos(th), 0.0), jnp.where(same, jnp.sin(th), 0.0)], axis=0) * norm
    tiles = c // LANE_TILE
    return pl.pallas_call(
        functools.partial(_fnet_kernel, n2=n2),
        out_shape=jax.ShapeDtypeStruct((b, l, c), MXU_DTYPE),
        grid=(b,),
        in_specs=[pl.BlockSpec((1, tiles, l, LANE_TILE), lambda bi: (bi, 0, 0, 0)),
                  _const_spec(fa.shape), _const_spec(g.shape), _const_spec(cm.shape)],
        out_specs=pl.BlockSpec((1, l, c), lambda bi: (bi, 0, 0)),
        scratch_shapes=[pltpu.VMEM((tiles, 2 * n1 * (n2 + PLANE_PAD_ROWS), LANE_TILE), F32),
                        pltpu.VMEM((tiles, l, LANE_TILE), F32)],
        compiler_params=_cparams(("parallel",)),
        name="fnet",
    )(f, fa, g.astype(MXU_DTYPE), cm.astype(MXU_DTYPE))


def _hy_filter_kernel(ze_ref, w1_ref, b1_ref, fr_ref, w2_ref, b2_ref, w3_ref, t_ref, dl_ref, o_ref):
    fr = fr_ref[...]
    h = jnp.sin(fr * (_dot_split(ze_ref[...], w1_ref[...]) + b1_ref[...]))
    h = jnp.sin(fr * (_dot_split(h, w2_ref[...]) + b2_ref[...]))
    h = _dot_split(h, w3_ref[...])
    c = dl_ref.shape[1]
    decay = jnp.exp(-t_ref[...] * dl_ref[...])
    hf = h[:, :c] * decay
    row = lax.broadcasted_iota(jnp.int32, (h.shape[0], 1), 0)
    hb = jnp.where(row == 0, 0.0, h[:, c:] * decay)
    nrm = jnp.sum(jnp.abs(hf), axis=0, keepdims=True) + jnp.sum(jnp.abs(hb), axis=0, keepdims=True)
    o_ref[:, :c] = hf / nrm
    o_ref[:, c:] = hb / nrm


def _hyena_filter_taps(l, hy):
    conv_w, conv_b, w1, b1, freq, w2, b2, w3, d_bias = hy
    emb = w1.shape[0]
    order = w1.shape[1]
    c = w3.shape[1] // 2
    t = jnp.linspace(0.0, 1.0, l, dtype=F32)[:, None]
    bands = (emb - 1) // 2
    fr = jnp.linspace(1e-4, bands - 1, bands, dtype=F32)
    ang = 2.0 * math.pi * jnp.arange(l, dtype=F32)[:, None] / l * fr
    z = jnp.concatenate([t, jnp.cos(ang), -jnp.sin(ang)], axis=-1)
    emb_pad = 128
    z = jnp.pad(z, ((0, 0), (0, emb_pad - emb)))
    w1p = jnp.pad(w1, ((0, emb_pad - emb), (0, 0)))
    min_decay = math.log(DECAY_TARGET_VAL) / SLOW_DECAY
    max_decay = math.log(DECAY_TARGET_VAL) / FAST_DECAY
    deltas = jnp.abs(jnp.linspace(min_decay, max_decay, c, dtype=F32))[None, :]
    args = (z, w1p, b1.reshape(1, order), freq.reshape(1, order), w2, b2.reshape(1, order), w3, t, deltas)
    return pl.pallas_call(
        _hy_filter_kernel,
        out_shape=jax.ShapeDtypeStruct((l, 2 * c), F32),
        grid=(1,),
        in_specs=[_const_spec(a.shape) for a in args],
        out_specs=_const_spec((l, 2 * c)),
        compiler_params=_cparams(("arbitrary",)),
        name="hyena_filter",
    )(*args)


def _hy_pre_kernel(u_ref, prev_ref, next_ref, w_ref, b_ref, x0_ref, z_ref):
    i = pl.program_id(1)
    tl = u_ref.shape[1]
    c = u_ref.shape[2] // 3
    u = u_ref[0].astype(F32)
    row = lax.broadcasted_iota(jnp.int32, (tl, 1), 0)
    halo = prev_ref.shape[1]
    before = jnp.where(i == 0, 0.0, prev_ref[0].astype(F32)[halo - 1:halo, :])
    after = jnp.where(i == pl.num_programs(1) - 1, 0.0, next_ref[0].astype(F32)[0:1, :])
    um = jnp.where(row == 0, before, pltpu.roll(u, 1, 0))
    up = jnp.where(row == tl - 1, after, pltpu.roll(u, tl - 1, 0))
    out = b_ref[...] + um * w_ref[0:1, :] + u * w_ref[1:2, :] + up * w_ref[2:3, :]
    _store_rows(x0_ref, (0,), slice(None), out[:, :c])
    _store_rows(z_ref, (0,), slice(None), out[:, c:2 * c] * out[:, 2 * c:])


def _hyena_pre(u, conv_w, conv_b):
    b, l, w = u.shape
    c = w // 3
    tl = min(1024, l)
    halo = 16
    nbh = tl // halo
    last_h = l // halo - 1
    tok = lambda width: pl.BlockSpec((1, tl, width), lambda bi, i: (bi, i, 0))
    tiles = c // LANE_TILE
    split = pl.BlockSpec((1, tiles, tl, LANE_TILE), lambda bi, i: (bi, 0, i, 0))
    split_shape = jax.ShapeDtypeStruct((b, tiles, l, LANE_TILE), F32)
    return pl.pallas_call(
        _hy_pre_kernel,
        out_shape=(split_shape, split_shape),
        grid=(b, l // tl),
        in_specs=[tok(w),
                  pl.BlockSpec((1, halo, w), lambda bi, i: (bi, jnp.maximum(i * nbh - 1, 0), 0)),
                  pl.BlockSpec((1, halo, w), lambda bi, i: (bi, jnp.minimum((i + 1) * nbh, last_h), 0)),
                  _const_spec(conv_w.shape), _const_spec((1, w))],
        out_specs=(split, split),
        compiler_params=_cparams(("parallel", "arbitrary")),
        name="hyena_pre",
    )(u, u, u, conv_w, conv_b.reshape(1, w))


def _hyena_spectrum(taps):
    l, c2 = taps.shape
    n = 2 * l
    n1, n2 = _split(n, True)
    k1p = _hermitian_planes(n1)
    a = _left_matmul(_coarse_fwd_table(n1, n1 // 2, k1p), taps.reshape(1, n1 // 2, n2 * c2), MXU_DTYPE)
    g, _ = _fine_tables(n1, n2, k1p)
    return _kf_mid(a.reshape(1, 2, k1p, n2, c2), g.astype(MXU_DTYPE))


def _hyena_kernel(x0_ref, z_ref, d_ref, fa_ref, g_ref, h_ref, kf_ref, fai_ref, o_ref, a_ref, y_ref, *, n2):
    l = z_ref.shape[2]
    n1h = l // n2
    k1p = g_ref.shape[0]
    sp = n2 + PLANE_PAD_ROWS
    fa = fa_ref[...]
    fai = fai_ref[...]

    def coarse(j, carry):
        zs = _load_rows(z_ref, (0,), pl.ds(j, n1h, stride=n2)).astype(MXU_DTYPE)
        _store_rows(a_ref, (), pl.ds(j, 2 * k1p, stride=sp), _dot(fa, zs))
        return carry

    lax.fori_loop(0, n2, coarse, 0, unroll=16)

    def fine(k1, carry):
        re_rows = _plane_rows(k1, sp, n2)
        im_rows = _plane_rows(k1p + k1, sp, n2)
        x = jnp.concatenate([_load_rows(a_ref, (), re_rows), _load_rows(a_ref, (), im_rows)], axis=0)
        zf = _dot(g_ref[k1], x.astype(MXU_DTYPE))
        zr, zi = zf[:n2], zf[n2:]
        kr = kf_ref[k1, :n2].astype(F32)
        ki = kf_ref[k1, n2:].astype(F32)
        yf = jnp.concatenate([zr * kr - zi * ki, zr * ki + zi * kr], axis=0).astype(MXU_DTYPE)
        bk = _dot(h_ref[k1], yf)
        _store_rows(a_ref, (), re_rows, bk[:n2])
        _store_rows(a_ref, (), im_rows, bk[n2:])
        return carry

    lax.fori_loop(0, k1p, fine, 0, unroll=8)

    def coarse_inv(j, carry):
        bs = _load_rows(a_ref, (), pl.ds(j, 2 * k1p, stride=sp)).astype(MXU_DTYPE)
        _store_rows(y_ref, (), pl.ds(j, n1h, stride=n2), _dot(fai, bs))
        return carry

    lax.fori_loop(0, n2, coarse_inv, 0, unroll=16)
    everything = slice(None)
    gated = _load_rows(x0_ref, (0,), everything) * (
        _load_rows(y_ref, (), everything) + d_ref[...] * _load_rows(z_ref, (0,), everything))
    o_ref[0] = gated.astype(o_ref.dtype)


def _hyena_gated_conv(x0, z, d_bias, kf):
    b, tiles, l, _ = z.shape
    c = tiles * LANE_TILE
    n = 2 * l
    n1, n2 = _split(n, True)
    k1p = kf.shape[0]
    fa = _coarse_fwd_table(n1, n1 // 2, k1p).astype(MXU_DTYPE)
    fai = _coarse_inv_table(n1, n1 // 2, n, k1p).astype(MXU_DTYPE)
    g, h = _fine_tables(n1, n2, k1p)
    seq = pl.BlockSpec((1, tiles, l, LANE_TILE), lambda bi: (bi, 0, 0, 0), pipeline_mode=pl.Buffered(1))
    resident = lambda a: pl.BlockSpec(a.shape, lambda bi: (0,) * a.ndim, pipeline_mode=pl.Buffered(1))
    return pl.pallas_call(
        functools.partial(_hyena_kernel, n2=n2),
        out_shape=jax.ShapeDtypeStruct((b, l, c), MXU_DTYPE),
        grid=(b,),
        in_specs=[seq, seq, _const_spec((1, c)), _const_spec(fa.shape), resident(g), resident(h), resident(kf),
                  _const_spec(fai.shape)],
        out_specs=pl.BlockSpec((1, l, c), lambda bi: (bi, 0, 0)),
        scratch_shapes=[pltpu.VMEM((tiles, 2 * k1p * (n2 + PLANE_PAD_ROWS), LANE_TILE), F32),
                        pltpu.VMEM((tiles, l, LANE_TILE), F32)],
        compiler_params=_cparams(("parallel",)),
        name="hyena_conv",
    )(x0, z, d_bias.reshape(1, c), fa, g.astype(MXU_DTYPE), h.astype(MXU_DTYPE), kf, fai)


def _channel_kernel(x_ref, att_ref, f_ref, hz_ref, gate1_ref, wo_ref, sh_ref, sc_ref, gate2_ref, g_ref,
                    w1_ref, w2_ref, fg_ref, o_ref, *, chunk, final_norm):
    a_w = att_ref.shape[2]
    f_w = f_ref.shape[2]
    mix = (_dot(att_ref[0], wo_ref[:a_w]) + _dot(f_ref[0], wo_ref[a_w:a_w + f_w])
           + _dot(hz_ref[0], wo_ref[a_w + f_w:]))
    x = x_ref[0] + gate1_ref[0] * mix
    h = (_rms(x, g_ref[...]) * (1.0 + sc_ref[0]) + sh_ref[0]).astype(MXU_DTYPE)
    acc = jnp.zeros(x.shape, F32)
    for c in range(w1_ref.shape[1] // chunk):
        a = jnp.maximum(_dot(h, w1_ref[:, c * chunk:(c + 1) * chunk]), 0.0)
        acc = acc + _dot((a * a).astype(MXU_DTYPE), w2_ref[c * chunk:(c + 1) * chunk, :])
    out = x + gate2_ref[0] * acc
    if final_norm:
        out = _rms(out, fg_ref[...])
    o_ref[0] = out


def _channel_mix(x, att, f, hz, gate1, w_out, shift, scale, gate2, norm_g, w1, w2, final_g, final_norm):
    b, l, d = x.shape
    tm = min(512, l)
    tok = lambda w: pl.BlockSpec((1, tm, w), lambda bi, i: (bi, i, 0))
    vec = pl.BlockSpec((1, 1, d), lambda bi, i: (bi, 0, 0))
    resident = lambda shape: pl.BlockSpec(shape, lambda bi, i: (0, 0), pipeline_mode=pl.Buffered(1))
    return pl.pallas_call(
        functools.partial(_channel_kernel, chunk=512, final_norm=final_norm),
        out_shape=jax.ShapeDtypeStruct(x.shape, F32),
        grid=(b, l // tm),
        in_specs=[tok(d), tok(att.shape[2]), tok(f.shape[2]), tok(hz.shape[2]), vec, resident(w_out.shape),
                  vec, vec, vec, _const_spec((1, d)), resident(w1.shape), resident(w2.shape),
                  _const_spec((1, d))],
        out_specs=tok(d),
        compiler_params=_cparams(("parallel", "arbitrary")),
        name="channel_mix",
    )(x, att, f, hz, gate1, w_out, shift, scale, gate2, norm_g, w1, w2, final_g)


def _rope_partner_perm():
    half = AXIS_ROPE_W // 2
    partner, sign = [], []
    for j in range(QK_ROPE_W):
        first = (j % AXIS_ROPE_W) < half
        partner.append(j + half if first else j - half)
        sign.append(-1.0 if first else 1.0)
    return partner, sign


def _rope_slabs(n_tokens, rotary):
    partner, sign = _rope_partner_perm()
    if rotary:
        rows = n_tokens // ROPE_GRID_W
        row = jnp.repeat(jnp.arange(rows, dtype=F32), ROPE_GRID_W)
        col = jnp.tile(jnp.arange(ROPE_GRID_W, dtype=F32), rows)
        inv = ROPE_THETA ** (-jnp.arange(0, AXIS_ROPE_W, 2, dtype=F32) / AXIS_ROPE_W)
        ang = jnp.concatenate([row[:, None] * inv, col[:, None] * inv], axis=-1)
        cos16, sin16 = jnp.cos(ang), jnp.sin(ang)
        half = AXIS_ROPE_W // 2
        cols = jnp.array([(j // AXIS_ROPE_W) * half + j % half for j in range(QK_ROPE_W)])
        cos32 = cos16[:, cols]
        sin32 = sin16[:, cols] * jnp.array(sign, F32)
    else:
        cos32 = jnp.ones((n_tokens, QK_ROPE_W), F32)
        sin32 = jnp.zeros((n_tokens, QK_ROPE_W), F32)
    pad = HEAD_SLAB - QK_NOPE_W - QK_ROPE_W
    cos_t = jnp.concatenate([jnp.ones((n_tokens, QK_NOPE_W), F32), cos32, jnp.zeros((n_tokens, pad), F32)], axis=1)
    sin_t = jnp.concatenate([jnp.zeros((n_tokens, QK_NOPE_W), F32), sin32, jnp.zeros((n_tokens, pad), F32)], axis=1)
    return cos_t, sin_t, cos_t.T, sin_t.T


def _layer_weights(w_in, q_g, kv_g, w_uq, w_ukv):
    partner, _ = _rope_partner_perm()
    partner = jnp.array(partner)
    d = w_in.shape[0]
    pad = HEAD_SLAB - QK_NOPE_W - QK_ROPE_W
    zeros = lambda rows, w: jnp.zeros((rows, w), w_in.dtype)
    kr = w_in[:, OFF_K_ROPE:OFF_FNET]
    w_in_aug = jnp.concatenate([
        w_in[:, :OFF_K_ROPE],
        zeros(d, QK_NOPE_W), kr, zeros(d, pad),
        zeros(d, QK_NOPE_W), kr[:, partner], zeros(d, pad),
        w_in[:, OFF_FNET:]], axis=1)
    wq = w_uq.reshape(Q_LORA_W, HEADS, QK_NOPE_W + QK_ROPE_W)
    wq_a = w_uq
    wq_b = wq[:, :, QK_NOPE_W:][:, :, partner].reshape(Q_LORA_W, HEADS * QK_ROPE_W)
    wkv = w_ukv.reshape(KV_LORA_W, HEADS, QK_NOPE_W + V_HEAD_W)
    wk = jnp.concatenate([wkv[:, :, :QK_NOPE_W], jnp.zeros((KV_LORA_W, HEADS, HEAD_SLAB - QK_NOPE_W), w_ukv.dtype)],
                         axis=2).reshape(KV_LORA_W, HEADS * HEAD_SLAB)
    wv = wkv[:, :, QK_NOPE_W:].reshape(KV_LORA_W, HEADS * V_HEAD_W)
    cast = lambda a: a.astype(MXU_DTYPE)
    return {"w_in": cast(w_in_aug), "q_g": q_g.reshape(1, -1), "kv_g": kv_g.reshape(1, -1),
            "wq_a_t": cast(wq_a.T), "wq_b_t": cast(wq_b.T), "wk": cast(wk), "wv_t": cast(wv.T)}


def _fourier_and_hyena(f, u, hy):
    conv_w, conv_b, d_bias = hy[0], hy[1], hy[8]
    fm = _fourier_mix(f)
    x0, z = _hyena_pre(u, conv_w, conv_b)
    kf = _hyena_spectrum(_hyena_filter_taps(u.shape[1], hy))
    hz = _hyena_gated_conv(x0, z, d_bias, kf)
    return fm, hz


def kernel(x, c, ctx, c_ctx, norm1_g, norm2_g, w_mod, b_mod, w_in, q_norm_g, kv_norm_g, w_uq, w_ukv,
           hy_conv_w, hy_conv_b, hy_w1, hy_b1, hy_freq, hy_w2, hy_b2, hy_w3, hy_d, w_out, w_mlp1,
           w_mlp2, final_norm_g):
    b, l, d = x.shape
    lc = ctx.shape[1]
    depth = w_mod.shape[0]
    rows = 16
    cc = jnp.concatenate([c, c_ctx[None, :], jnp.zeros((rows - b - 1, d), F32)], axis=0)
    mod = _modulation(cc, w_mod, b_mod)
    rope_x = _rope_slabs(l, True)
    rope_c = _rope_slabs(lc, False)
    final_g = final_norm_g.reshape(1, d)
    xc = ctx
    for li in range(depth):
        last = li == depth - 1
        mx = mod[li, :b].reshape(b, 1, N_MODULATION, d)
        mc = jnp.broadcast_to(mod[li, b].reshape(1, 1, N_MODULATION, d), (b, 1, N_MODULATION, d))
        sh1, sc1, g1, sh2, sc2, g2 = [mx[:, :, i] for i in range(N_MODULATION)]
        csh1, csc1, cg1, csh2, csc2, cg2 = [mc[:, :, i] for i in range(N_MODULATION)]
        hy = (hy_conv_w[li], hy_conv_b[li], hy_w1[li], hy_b1[li], hy_freq[li], hy_w2[li], hy_b2[li],
              hy_w3[li], hy_d[li])
        lw = _layer_weights(w_in[li], q_norm_g[li], kv_norm_g[li], w_uq[li], w_ukv[li])
        n1g = norm1_g[li].reshape(1, d)
        n2g = norm2_g[li].reshape(1, d)
        wo = w_out[li].astype(MXU_DTYPE)
        w1 = w_mlp1[li].astype(MXU_DTYPE)
        w2 = w_mlp2[li].astype(MXU_DTYPE)

        q_x, k_x, v_x, f_x, u_x = _in_proj(x, sh1, sc1, n1g, lw, rope_x)
        q_c, k_c, v_c, f_c, u_c = _in_proj(xc, csh1, csc1, n1g, lw, rope_c)
        att_x = _attention(q_x, [(k_x, v_x), (k_c, v_c)])
        fm_x, hz_x = _fourier_and_hyena(f_x, u_x, hy)
        x = _channel_mix(x, att_x, fm_x, hz_x, g1, wo, sh2, sc2, g2, n2g, w1, w2, final_g, last)
        if not last:
            att_c = _attention(q_c, [(k_c, v_c)])
            fm_c, hz_c = _fourier_and_hyena(f_c, u_c, hy)
            xc = _channel_mix(xc, att_c, fm_c, hz_c, cg1, wo, csh2, csc2, cg2, n2g, w1, w2, final_g, False)
    return x
```

```python
import functools
import math

import jax
import jax.numpy as jnp
from jax import lax
from jax.experimental import pallas as pl
from jax.experimental.pallas import tpu as pltpu

F32 = jnp.float32
MXU_DTYPE = jnp.bfloat16

HEADS = 8
QK_NOPE_W = 64
QK_ROPE_W = 32
V_HEAD_W = 64
Q_LORA_W = 384
KV_LORA_W = 256
AXIS_ROPE_W = QK_ROPE_W // 2
ROPE_THETA = 10000.0
ROPE_GRID_W = 64
FNET_GROUP = 64
FNET_WIDTH = 256
HYENA_WIDTH = 256
N_MODULATION = 6
NORM_EPS = 1e-6
DECAY_TARGET_VAL = 1e-2
FAST_DECAY = 0.3
SLOW_DECAY = 1.5

HEAD_SLAB = 128
V_ONES_ROWS = 16
V_SLAB = V_HEAD_W + V_ONES_ROWS
OFF_KV_LORA = Q_LORA_W
OFF_K_ROPE = OFF_KV_LORA + KV_LORA_W
OFF_FNET = OFF_K_ROPE + QK_ROPE_W
OFF_HYENA = OFF_FNET + FNET_WIDTH

AUG_KR = OFF_K_ROPE
AUG_KR_SWAP = AUG_KR + HEAD_SLAB
AUG_F = AUG_KR_SWAP + HEAD_SLAB
AUG_H = AUG_F + FNET_WIDTH
AUG_W = AUG_H + 3 * HYENA_WIDTH

V7X_VMEM_LIMIT = 56 * 1024 * 1024


def _cparams(sem):
    return pltpu.CompilerParams(dimension_semantics=sem, vmem_limit_bytes=V7X_VMEM_LIMIT)


def _dot(a, b):
    return jnp.dot(a, b, preferred_element_type=F32)


def _dot_split(a, b):
    a_hi = a.astype(MXU_DTYPE)
    b_hi = b.astype(MXU_DTYPE)
    a_lo = (a - a_hi.astype(F32)).astype(MXU_DTYPE)
    b_lo = (b - b_hi.astype(F32)).astype(MXU_DTYPE)
    return _dot(a_hi, b_hi) + (_dot(a_hi, b_lo) + _dot(a_lo, b_hi))


def _rms(x, g):
    return x * lax.rsqrt(jnp.mean(x * x, axis=-1, keepdims=True) + NORM_EPS) * g


def _const_spec(shape):
    n = len(shape)
    return pl.BlockSpec(shape, lambda *_: (0,) * n)


def _mod_kernel(c_ref, w_ref, b_ref, o_ref):
    c = c_ref[...]
    s = c / (1.0 + jnp.exp(-c))
    o_ref[0] = _dot(s.astype(MXU_DTYPE), w_ref[0].astype(MXU_DTYPE)) + b_ref[0]


def _modulation(cc, w_mod, b_mod):
    depth, d, n = w_mod.shape
    r = cc.shape[0]
    tn = 1024
    return pl.pallas_call(
        _mod_kernel,
        out_shape=jax.ShapeDtypeStruct((depth, r, n), F32),
        grid=(depth, n // tn),
        in_specs=[pl.BlockSpec((r, d), lambda l, j: (0, 0)),
                  pl.BlockSpec((1, d, tn), lambda l, j: (l, 0, j)),
                  pl.BlockSpec((1, 1, tn), lambda l, j: (l, 0, j))],
        out_specs=pl.BlockSpec((1, r, tn), lambda l, j: (l, 0, j)),
        compiler_params=_cparams(("arbitrary", "arbitrary")),
        name="modulation",
    )(cc, w_mod, b_mod.reshape(depth, 1, n))


def _in_proj_kernel(x_ref, sh_ref, sc_ref, g_ref, win_ref, qg_ref, kvg_ref, wqa_ref, wqb_ref,
                    wk_ref, wv_ref, cos_ref, sin_ref, cos_t_ref, sin_t_ref,
                    qt_ref, k_ref, vt_ref, f_ref, u_ref, *, q_scale):
    h = _rms(x_ref[0], g_ref[...]) * (1.0 + sc_ref[0]) + sh_ref[0]
    p = _dot(h.astype(MXU_DTYPE), win_ref[...])
    cq_t = _rms(p[:, :Q_LORA_W], qg_ref[...]).T.astype(MXU_DTYPE)
    qa_t = _dot(wqa_ref[...], cq_t)
    qb_t = _dot(wqb_ref[...], cq_t)
    rope = slice(QK_NOPE_W, QK_NOPE_W + QK_ROPE_W)
    cos_r = cos_t_ref[rope, :]
    sin_r = sin_t_ref[rope, :]
    ckv = _rms(p[:, OFF_KV_LORA:OFF_K_ROPE], kvg_ref[...])
    kn = _dot(ckv.astype(MXU_DTYPE), wk_ref[...])
    kr = p[:, AUG_KR:AUG_KR_SWAP] * cos_ref[...] + p[:, AUG_KR_SWAP:AUG_F] * sin_ref[...]
    head_w = QK_NOPE_W + QK_ROPE_W
    pad_rows = jnp.zeros((HEAD_SLAB - head_w, cq_t.shape[1]), qt_ref.dtype)
    for hd in range(HEADS):
        hs = slice(hd * HEAD_SLAB, (hd + 1) * HEAD_SLAB)
        base = hd * HEAD_SLAB
        q_nope = qa_t[hd * head_w:hd * head_w + QK_NOPE_W]
        q_rope = (qa_t[hd * head_w + QK_NOPE_W:(hd + 1) * head_w] * cos_r
                  + qb_t[hd * QK_ROPE_W:(hd + 1) * QK_ROPE_W] * sin_r)
        qt_ref[0, base:base + QK_NOPE_W, :] = (q_nope * q_scale).astype(qt_ref.dtype)
        qt_ref[0, base + QK_NOPE_W:base + head_w, :] = (q_rope * q_scale).astype(qt_ref.dtype)
        qt_ref[0, base + head_w:base + HEAD_SLAB, :] = pad_rows
        k_ref[0, :, hs] = (kn[:, hs] + kr).astype(k_ref.dtype)
    vt = _dot(wv_ref[...], ckv.T.astype(MXU_DTYPE)).astype(vt_ref.dtype)
    ones = jnp.ones((V_ONES_ROWS, vt.shape[1]), vt_ref.dtype)
    for hd in range(HEADS):
        vt_ref[0, hd * V_SLAB:hd * V_SLAB + V_HEAD_W, :] = vt[hd * V_HEAD_W:(hd + 1) * V_HEAD_W]
        vt_ref[0, hd * V_SLAB + V_HEAD_W:(hd + 1) * V_SLAB, :] = ones
    _store_rows(f_ref, (0,), slice(None), p[:, AUG_F:AUG_H])
    u_ref[0] = p[:, AUG_H:AUG_W].astype(u_ref.dtype)


def _in_proj(x, shift, scale, norm_g, lw, rope):
    b, l, d = x.shape
    tm = min(512, l)
    q_scale = math.log2(math.e) / math.sqrt(QK_NOPE_W + QK_ROPE_W)
    tok = lambda w: pl.BlockSpec((1, tm, w), lambda bi, i: (bi, i, 0))
    tok_t = lambda w: pl.BlockSpec((1, w, tm), lambda bi, i: (bi, 0, i))
    vec = pl.BlockSpec((1, 1, d), lambda bi, i: (bi, 0, 0))
    tab = pl.BlockSpec((tm, HEAD_SLAB), lambda bi, i: (i, 0))
    tab_t = pl.BlockSpec((HEAD_SLAB, tm), lambda bi, i: (0, i))
    qk_w = HEADS * HEAD_SLAB
    v_w = HEADS * V_HEAD_W
    return pl.pallas_call(
        functools.partial(_in_proj_kernel, q_scale=q_scale),
        out_shape=(jax.ShapeDtypeStruct((b, qk_w, l), MXU_DTYPE),
                   jax.ShapeDtypeStruct((b, l, qk_w), MXU_DTYPE),
                   jax.ShapeDtypeStruct((b, HEADS * V_SLAB, l), MXU_DTYPE),
                   jax.ShapeDtypeStruct((b, FNET_WIDTH // LANE_TILE, l, LANE_TILE), F32),
                   jax.ShapeDtypeStruct((b, l, 3 * HYENA_WIDTH), MXU_DTYPE)),
        grid=(b, l // tm),
        in_specs=[tok(d), vec, vec, _const_spec((1, d)), _const_spec((d, AUG_W)),
                  _const_spec((1, Q_LORA_W)), _const_spec((1, KV_LORA_W)),
                  _const_spec(lw["wq_a_t"].shape), _const_spec(lw["wq_b_t"].shape),
                  _const_spec((KV_LORA_W, qk_w)), _const_spec((v_w, KV_LORA_W)), tab, tab, tab_t, tab_t],
        out_specs=(tok_t(qk_w), tok(qk_w), tok_t(HEADS * V_SLAB),
                   pl.BlockSpec((1, FNET_WIDTH // LANE_TILE, tm, LANE_TILE), lambda bi, i: (bi, 0, i, 0)),
                   tok(3 * HYENA_WIDTH)),
        compiler_params=_cparams(("parallel", "arbitrary")),
        name="in_proj",
    )(x, shift, scale, norm_g, lw["w_in"], lw["q_g"], lw["kv_g"], lw["wq_a_t"], lw["wq_b_t"],
      lw["wk"], lw["wv_t"], *rope)


def _attn_kernel(*refs, nseg, tk):
    qt_ref = refs[0]
    segs = [(refs[1 + 2 * s], refs[2 + 2 * s]) for s in range(nseg)]
    o_ref, s_ref, p_ref, ot_ref = refs[1 + 2 * nseg:]
    tq = s_ref.shape[2]
    heads = qt_ref.shape[1] // HEAD_SLAB
    units = [(part, hd) for part in range(qt_ref.shape[2] // tq) for hd in range(heads)]

    sub = 256
    chunks = []
    off = 0
    for si, (k_ref, _) in enumerate(segs):
        s_len = k_ref.shape[1]
        chunks += [(si, lo, off + lo, min(tk, s_len - lo)) for lo in range(0, s_len, tk)]
        off += s_len

    def scores_chunk(u, chunk, mx):
        si, lo, row, n = chunk
        part, hd = units[u]
        qt = qt_ref[0, hd * HEAD_SLAB:(hd + 1) * HEAD_SLAB, part * tq:(part + 1) * tq]
        s = _dot(segs[si][0][0, lo:lo + n, hd * HEAD_SLAB:(hd + 1) * HEAD_SLAB], qt)
        s_ref[u % 2, row:row + n, :] = s
        return jnp.maximum(mx, jnp.max(s, axis=0, keepdims=True))

    def probs_chunk(u, chunk, mx):
        _, _, row, n = chunk
        for r in range(row, row + n, sub):
            m = min(sub, row + n - r)
            p_ref[u % 2, r:r + m, :] = jnp.exp2(s_ref[u % 2, r:r + m, :] - mx).astype(p_ref.dtype)

    def values_chunk(u, chunk, acc):
        si, lo, row, n = chunk
        hd = units[u][1]
        for r in range(0, n, sub):
            m = min(sub, n - r)
            acc = acc + _dot(segs[si][1][0, hd * V_SLAB:(hd + 1) * V_SLAB, lo + r:lo + r + m],
                             p_ref[u % 2, row + r:row + r + m, :])
        return acc

    col_max = {}
    for step in range(len(units) + 2):
        mx = jnp.full((1, tq), -jnp.inf, F32)
        acc = jnp.zeros((V_SLAB, tq), F32)
        for chunk in chunks:
            if step < len(units):
                mx = scores_chunk(step, chunk, mx)
            if 1 <= step <= len(units):
                probs_chunk(step - 1, chunk, col_max[step - 1])
            if step >= 2:
                acc = values_chunk(step - 2, chunk, acc)
        col_max[step] = mx
        if step >= 2:
            part, hd = units[step - 2]
            ot_ref[hd * V_HEAD_W:(hd + 1) * V_HEAD_W, part * tq:(part + 1) * tq] = (
                acc[:V_HEAD_W] / acc[V_HEAD_W:V_HEAD_W + 1])
    o_ref[0] = ot_ref[...].T.astype(o_ref.dtype)


def _attention(qt, kv_segs):
    b, qk_w, l = qt.shape
    unit_q = min(256, l)
    tq = min(2 * unit_q, l)
    out_w = HEADS * V_HEAD_W
    in_specs = [pl.BlockSpec((1, qk_w, tq), lambda bi, i: (bi, 0, i))]
    args = [qt]
    for k, vt in kv_segs:
        s_len = k.shape[1]
        in_specs.append(pl.BlockSpec((1, s_len, qk_w), lambda bi, i: (bi, 0, 0)))
        in_specs.append(pl.BlockSpec((1, HEADS * V_SLAB, s_len), lambda bi, i: (bi, 0, 0)))
        args += [k, vt]
    s_total = sum(k.shape[1] for k, _ in kv_segs)
    return pl.pallas_call(
        functools.partial(_attn_kernel, nseg=len(kv_segs), tk=512),
        out_shape=jax.ShapeDtypeStruct((b, l, out_w), MXU_DTYPE),
        grid=(b, l // tq),
        in_specs=in_specs,
        out_specs=pl.BlockSpec((1, tq, out_w), lambda bi, i: (bi, i, 0)),
        scratch_shapes=[pltpu.VMEM((2, s_total, unit_q), F32), pltpu.VMEM((2, s_total, unit_q), MXU_DTYPE),
                        pltpu.VMEM((out_w, tq), F32)],
        compiler_params=_cparams(("parallel", "arbitrary")),
        name="attention",
    )(*args)


def _split(n, zero_padded):
    n1 = 1 << (int(math.log2(n)) // 2)
    n2 = n // n1
    if zero_padded and n1 // 2 < 16 and n2 > n1:
        n1, n2 = n2, n1
    return n1, n2


def _angles(rows, cols, n):
    idx = (rows[:, None] * cols[None, :]) % n
    return idx.astype(F32) * (2.0 * math.pi / n)


def _coarse_fwd_table(n1, k_in, k1_count):
    th = _angles(jnp.arange(k1_count, dtype=jnp.int32), jnp.arange(k_in, dtype=jnp.int32), n1)
    return jnp.concatenate([jnp.cos(th), -jnp.sin(th)], axis=0)


def _coarse_inv_table(n1, t_out, n, k1_count):
    k1 = jnp.arange(k1_count, dtype=jnp.int32)
    th = _angles(jnp.arange(t_out, dtype=jnp.int32), k1, n1)
    w = jnp.where((k1 == 0) | (k1 == n1 // 2), 1.0, jnp.where(k1 < n1 // 2, 2.0, 0.0)) * (1.0 / n)
    return jnp.concatenate([jnp.cos(th) * w, -jnp.sin(th) * w], axis=1)


def _fine_tables(n1, n2, k1_count):
    n = n1 * n2
    k = (jnp.arange(k1_count, dtype=jnp.int32)[:, None] + n1 * jnp.arange(n2, dtype=jnp.int32)[None, :])
    idx = (k[:, :, None] * jnp.arange(n2, dtype=jnp.int32)[None, None, :]) % n
    ph = idx.astype(F32) * (2.0 * math.pi / n)
    c, s = jnp.cos(ph), jnp.sin(ph)
    fwd = jnp.concatenate([jnp.concatenate([c, s], axis=2), jnp.concatenate([-s, c], axis=2)], axis=1)
    return fwd, jnp.swapaxes(fwd, 1, 2)


def _hermitian_planes(n1):
    return -(-(n1 // 2 + 1) // 8) * 8


PLANE_PAD_ROWS = 8


def _lmm_kernel(t_ref, x_ref, o_ref):
    o_ref[0] = _dot(t_ref[...], x_ref[0].astype(MXU_DTYPE)).astype(o_ref.dtype)


def _left_matmul(table, x, out_dtype):
    b, k, n = x.shape
    m = table.shape[0]
    tc = min(2048, n)
    return pl.pallas_call(
        _lmm_kernel,
        out_shape=jax.ShapeDtypeStruct((b, m, n), out_dtype),
        grid=(b, n // tc),
        in_specs=[_const_spec((m, k)), pl.BlockSpec((1, k, tc), lambda bi, j: (bi, 0, j))],
        out_specs=pl.BlockSpec((1, m, tc), lambda bi, j: (bi, 0, j)),
        compiler_params=_cparams(("parallel", "arbitrary")),
        name="dft_coarse",
    )(table.astype(MXU_DTYPE), x)


def _kf_mid_kernel(x_ref, g_ref, o_ref):
    n2 = x_ref.shape[3]
    c = o_ref.shape[2]
    for j in range(x_ref.shape[2]):
        zf = _dot(g_ref[j], x_ref[0, :, j].reshape(2 * n2, x_ref.shape[4]))
        o_ref[j, :n2] = (zf[:n2, :c] + zf[:n2, c:]).astype(o_ref.dtype)
        o_ref[j, n2:] = (zf[n2:, :c] - zf[n2:, c:]).astype(o_ref.dtype)


def _kf_mid(a5, g):
    _, _, k1p, n2, c2 = a5.shape
    c = c2 // 2
    t1 = 8
    return pl.pallas_call(
        _kf_mid_kernel,
        out_shape=jax.ShapeDtypeStruct((k1p, 2 * n2, c), MXU_DTYPE),
        grid=(k1p // t1,),
        in_specs=[pl.BlockSpec((1, 2, t1, n2, c2), lambda i: (0, 0, i, 0, 0)),
                  pl.BlockSpec((t1, 2 * n2, 2 * n2), lambda i: (i, 0, 0))],
        out_specs=pl.BlockSpec((t1, 2 * n2, c), lambda i: (i, 0, 0)),
        compiler_params=_cparams(("arbitrary",)),
        name="hyena_filter_spectrum",
    )(a5, g)


LANE_TILE = 128


def _split_lanes(a):
    return jnp.stack([a[..., i * LANE_TILE:(i + 1) * LANE_TILE] for i in range(a.shape[-1] // LANE_TILE)], axis=-3)


def _load_rows(ref, pre, rows):
    return jnp.concatenate([ref[pre + (t, rows, slice(None))] for t in range(ref.shape[len(pre)])], axis=1)


def _store_rows(ref, pre, rows, val):
    for t in range(ref.shape[len(pre)]):
        ref[pre + (t, rows, slice(None))] = val[:, t * LANE_TILE:(t + 1) * LANE_TILE]


def _plane_rows(plane, sp, n2):
    return pl.ds(pl.multiple_of(plane * sp, 8), n2)


def _fnet_kernel(f_ref, fa_ref, g_ref, cm_ref, o_ref, a_ref, r_ref, *, n2):
    l = f_ref.shape[2]
    n1 = l // n2
    sp = n2 + PLANE_PAD_ROWS
    fa = fa_ref[...]

    def coarse(j, carry):
        zs = _load_rows(f_ref, (0,), pl.ds(j, n1, stride=n2)).astype(MXU_DTYPE)
        _store_rows(a_ref, (), pl.ds(j, 2 * n1, stride=sp), _dot(fa, zs))
        return carry

    lax.fori_loop(0, n2, coarse, 0, unroll=16)

    def fine(k1, carry):
        x = jnp.concatenate([_load_rows(a_ref, (), _plane_rows(k1, sp, n2)),
                             _load_rows(a_ref, (), _plane_rows(n1 + k1, sp, n2))], axis=0)
        pf = _dot(g_ref[k1], x.astype(MXU_DTYPE)).astype(MXU_DTYPE)
        res = _dot(jnp.concatenate([pf[:n2], pf[n2:]], axis=1), cm_ref[...])
        _store_rows(r_ref, (), pl.ds(k1, n2, stride=n1), res)
        return carry

    lax.fori_loop(0, n1, fine, 0, unroll=16)
    o_ref[0] = _load_rows(r_ref, (), slice(None)).astype(o_ref.dtype)


def _fourier_mix(f):
    b, tiles, l, _ = f.shape
    c = tiles * LANE_TILE
    n1, n2 = _split(l, False)
    fa = _coarse_fwd_table(n1, n1, n1).astype(MXU_DTYPE)
    g, _ = _fine_tables(n1, n2, n1)
    ch = jnp.arange(c, dtype=jnp.int32)
    th = _angles(ch % FNET_GROUP, ch % FNET_GROUP, FNET_GROUP)
    same = (ch[:, None] // FNET_GROUP) == (ch[None, :] // FNET_GROUP)
    norm = 1.0 / math.sqrt(l * FNET_GROUP)
    cm = jnp.concatenate([jnp.where(same, jnp.cos(th), 0.0), jnp.where(same, jnp.sin(th), 0.0)], axis=0) * norm
    tiles = c // LANE_TILE
    return pl.pallas_call(
        functools.partial(_fnet_kernel, n2=n2),
        out_shape=jax.ShapeDtypeStruct((b, l, c), MXU_DTYPE),
        grid=(b,),
        in_specs=[pl.BlockSpec((1, tiles, l, LANE_TILE), lambda bi: (bi, 0, 0, 0)),
                  _const_spec(fa.shape), _const_spec(g.shape), _const_spec(cm.shape)],
        out_specs=pl.BlockSpec((1, l, c), lambda bi: (bi, 0, 0)),
        scratch_shapes=[pltpu.VMEM((tiles, 2 * n1 * (n2 + PLANE_PAD_ROWS), LANE_TILE), F32),
                        pltpu.VMEM((tiles, l, LANE_TILE), F32)],
        compiler_params=_cparams(("parallel",)),
        name="fnet",
    )(f, fa, g.astype(MXU_DTYPE), cm.astype(MXU_DTYPE))


def _hy_filter_kernel(ze_ref, w1_ref, b1_ref, fr_ref, w2_ref, b2_ref, w3_ref, t_ref, dl_ref, o_ref):
    fr = fr_ref[...]
    h = jnp.sin(fr * (_dot_split(ze_ref[...], w1_ref[...]) + b1_ref[...]))
    h = jnp.sin(fr * (_dot_split(h, w2_ref[...]) + b2_ref[...]))
    h = _dot_split(h, w3_ref[...])
    c = dl_ref.shape[1]
    decay = jnp.exp(-t_ref[...] * dl_ref[...])
    hf = h[:, :c] * decay
    row = lax.broadcasted_iota(jnp.int32, (h.shape[0], 1), 0)
    hb = jnp.where(row == 0, 0.0, h[:, c:] * decay)
    nrm = jnp.sum(jnp.abs(hf), axis=0, keepdims=True) + jnp.sum(jnp.abs(hb), axis=0, keepdims=True)
    o_ref[:, :c] = hf / nrm
    o_ref[:, c:] = hb / nrm


def _hyena_filter_taps(l, hy):
    conv_w, conv_b, w1, b1, freq, w2, b2, w3, d_bias = hy
    emb = w1.shape[0]
    order = w1.shape[1]
    c = w3.shape[1] // 2
    t = jnp.linspace(0.0, 1.0, l, dtype=F32)[:, None]
    bands = (emb - 1) // 2
    fr = jnp.linspace(1e-4, bands - 1, bands, dtype=F32)
    ang = 2.0 * math.pi * jnp.arange(l, dtype=F32)[:, None] / l * fr
    z = jnp.concatenate([t, jnp.cos(ang), -jnp.sin(ang)], axis=-1)
    emb_pad = 128
    z = jnp.pad(z, ((0, 0), (0, emb_pad - emb)))
    w1p = jnp.pad(w1, ((0, emb_pad - emb), (0, 0)))
    min_decay = math.log(DECAY_TARGET_VAL) / SLOW_DECAY
    max_decay = math.log(DECAY_TARGET_VAL) / FAST_DECAY
    deltas = jnp.abs(jnp.linspace(min_decay, max_decay, c, dtype=F32))[None, :]
    args = (z, w1p, b1.reshape(1, order), freq.reshape(1, order), w2, b2.reshape(1, order), w3, t, deltas)
    return pl.pallas_call(
        _hy_filter_kernel,
        out_shape=jax.ShapeDtypeStruct((l, 2 * c), F32),
        grid=(1,),
        in_specs=[_const_spec(a.shape) for a in args],
        out_specs=_const_spec((l, 2 * c)),
        compiler_params=_cparams(("arbitrary",)),
        name="hyena_filter",
    )(*args)


def _hy_pre_kernel(u_ref, prev_ref, next_ref, w_ref, b_ref, x0_ref, z_ref):
    i = pl.program_id(1)
    tl = u_ref.shape[1]
    c = u_ref.shape[2] // 3
    u = u_ref[0].astype(F32)
    row = lax.broadcasted_iota(jnp.int32, (tl, 1), 0)
    halo = prev_ref.shape[1]
    before = jnp.where(i == 0, 0.0, prev_ref[0].astype(F32)[halo - 1:halo, :])
    after = jnp.where(i == pl.num_programs(1) - 1, 0.0, next_ref[0].astype(F32)[0:1, :])
    um = jnp.where(row == 0, before, pltpu.roll(u, 1, 0))
    up = jnp.where(row == tl - 1, after, pltpu.roll(u, tl - 1, 0))
    out = b_ref[...] + um * w_ref[0:1, :] + u * w_ref[1:2, :] + up * w_ref[2:3, :]
    _store_rows(x0_ref, (0,), slice(None), out[:, :c])
    _store_rows(z_ref, (0,), slice(None), out[:, c:2 * c] * out[:, 2 * c:])


def _hyena_pre(u, conv_w, conv_b):
    b, l, w = u.shape
    c = w // 3
    tl = min(1024, l)
    halo = 16
    nbh = tl // halo
    last_h = l // halo - 1
    tok = lambda width: pl.BlockSpec((1, tl, width), lambda bi, i: (bi, i, 0))
    tiles = c // LANE_TILE
    split = pl.BlockSpec((1, tiles, tl, LANE_TILE), lambda bi, i: (bi, 0, i, 0))
    split_shape = jax.ShapeDtypeStruct((b, tiles, l, LANE_TILE), F32)
    return pl.pallas_call(
        _hy_pre_kernel,
        out_shape=(split_shape, split_shape),
        grid=(b, l // tl),
        in_specs=[tok(w),
                  pl.BlockSpec((1, halo, w), lambda bi, i: (bi, jnp.maximum(i * nbh - 1, 0), 0)),
                  pl.BlockSpec((1, halo, w), lambda bi, i: (bi, jnp.minimum((i + 1) * nbh, last_h), 0)),
                  _const_spec(conv_w.shape), _const_spec((1, w))],
        out_specs=(split, split),
        compiler_params=_cparams(("parallel", "arbitrary")),
        name="hyena_pre",
    )(u, u, u, conv_w, conv_b.reshape(1, w))


def _hyena_spectrum(taps):
    l, c2 = taps.shape
    n = 2 * l
    n1, n2 = _split(n, True)
    k1p = _hermitian_planes(n1)
    a = _left_matmul(_coarse_fwd_table(n1, n1 // 2, k1p), taps.reshape(1, n1 // 2, n2 * c2), MXU_DTYPE)
    g, _ = _fine_tables(n1, n2, k1p)
    return _kf_mid(a.reshape(1, 2, k1p, n2, c2), g.astype(MXU_DTYPE))


def _hyena_kernel(x0_ref, z_ref, d_ref, fa_ref, g_ref, h_ref, kf_ref, fai_ref, o_ref, a_ref, y_ref, *, n2):
    l = z_ref.shape[2]
    n1h = l // n2
    k1p = g_ref.shape[0]
    sp = n2 + PLANE_PAD_ROWS
    fa = fa_ref[...]
    fai = fai_ref[...]

    def coarse(j, carry):
        zs = _load_rows(z_ref, (0,), pl.ds(j, n1h, stride=n2)).astype(MXU_DTYPE)
        _store_rows(a_ref, (), pl.ds(j, 2 * k1p, stride=sp), _dot(fa, zs))
        return carry

    lax.fori_loop(0, n2, coarse, 0, unroll=16)

    def fine(k1, carry):
        re_rows = _plane_rows(k1, sp, n2)
        im_rows = _plane_rows(k1p + k1, sp, n2)
        x = jnp.concatenate([_load_rows(a_ref, (), re_rows), _load_rows(a_ref, (), im_rows)], axis=0)
        zf = _dot(g_ref[k1], x.astype(MXU_DTYPE))
        zr, zi = zf[:n2], zf[n2:]
        kr = kf_ref[k1, :n2].astype(F32)
        ki = kf_ref[k1, n2:].astype(F32)
        yf = jnp.concatenate([zr * kr - zi * ki, zr * ki + zi * kr], axis=0).astype(MXU_DTYPE)
        bk = _dot(h_ref[k1], yf)
        _store_rows(a_ref, (), re_rows, bk[:n2])
        _store_rows(a_ref, (), im_rows, bk[n2:])
        return carry

    lax.fori_loop(0, k1p, fine, 0, unroll=8)

    def coarse_inv(j, carry):
        bs = _load_rows(a_ref, (), pl.ds(j, 2 * k1p, stride=sp)).astype(MXU_DTYPE)
        _store_rows(y_ref, (), pl.ds(j, n1h, stride=n2), _dot(fai, bs))
        return carry

    lax.fori_loop(0, n2, coarse_inv, 0, unroll=16)
    everything = slice(None)
    gated = _load_rows(x0_ref, (0,), everything) * (
        _load_rows(y_ref, (), everything) + d_ref[...] * _load_rows(z_ref, (0,), everything))
    o_ref[0] = gated.astype(o_ref.dtype)


def _hyena_gated_conv(x0, z, d_bias, kf):
    b, tiles, l, _ = z.shape
    c = tiles * LANE_TILE
    n = 2 * l
    n1, n2 = _split(n, True)
    k1p = kf.shape[0]
    fa = _coarse_fwd_table(n1, n1 // 2, k1p).astype(MXU_DTYPE)
    fai = _coarse_inv_table(n1, n1 // 2, n, k1p).astype(MXU_DTYPE)
    g, h = _fine_tables(n1, n2, k1p)
    seq = pl.BlockSpec((1, tiles, l, LANE_TILE), lambda bi: (bi, 0, 0, 0), pipeline_mode=pl.Buffered(1))
    resident = lambda a: pl.BlockSpec(a.shape, lambda bi: (0,) * a.ndim, pipeline_mode=pl.Buffered(1))
    return pl.pallas_call(
        functools.partial(_hyena_kernel, n2=n2),
        out_shape=jax.ShapeDtypeStruct((b, l, c), MXU_DTYPE),
        grid=(b,),
        in_specs=[seq, seq, _const_spec((1, c)), _const_spec(fa.shape), resident(g), resident(h), resident(kf),
                  _const_spec(fai.shape)],
        out_specs=pl.BlockSpec((1, l, c), lambda bi: (bi, 0, 0)),
        scratch_shapes=[pltpu.VMEM((tiles, 2 * k1p * (n2 + PLANE_PAD_ROWS), LANE_TILE), F32),
                        pltpu.VMEM((tiles, l, LANE_TILE), F32)],
        compiler_params=_cparams(("parallel",)),
        name="hyena_conv",
    )(x0, z, d_bias.reshape(1, c), fa, g.astype(MXU_DTYPE), h.astype(MXU_DTYPE), kf, fai)


def _channel_kernel(x_ref, att_ref, f_ref, hz_ref, gate1_ref, wo_ref, sh_ref, sc_ref, gate2_ref, g_ref,
                    w1_ref, w2_ref, fg_ref, o_ref, *, chunk, final_norm):
    a_w = att_ref.shape[2]
    f_w = f_ref.shape[2]
    mix = (_dot(att_ref[0], wo_ref[:a_w]) + _dot(f_ref[0], wo_ref[a_w:a_w + f_w])
           + _dot(hz_ref[0], wo_ref[a_w + f_w:]))
    x = x_ref[0] + gate1_ref[0] * mix
    h = (_rms(x, g_ref[...]) * (1.0 + sc_ref[0]) + sh_ref[0]).astype(MXU_DTYPE)
    acc = jnp.zeros(x.shape, F32)
    for c in range(w1_ref.shape[1] // chunk):
        a = jnp.maximum(_dot(h, w1_ref[:, c * chunk:(c + 1) * chunk]), 0.0)
        acc = acc + _dot((a * a).astype(MXU_DTYPE), w2_ref[c * chunk:(c + 1) * chunk, :])
    out = x + gate2_ref[0] * acc
    if final_norm:
        out = _rms(out, fg_ref[...])
    o_ref[0] = out


def _channel_mix(x, att, f, hz, gate1, w_out, shift, scale, gate2, norm_g, w1, w2, final_g, final_norm):
    b, l, d = x.shape
    tm = min(512, l)
    tok = lambda w: pl.BlockSpec((1, tm, w), lambda bi, i: (bi, i, 0))
    vec = pl.BlockSpec((1, 1, d), lambda bi, i: (bi, 0, 0))
    resident = lambda shape: pl.BlockSpec(shape, lambda bi, i: (0, 0), pipeline_mode=pl.Buffered(1))
    return pl.pallas_call(
        functools.partial(_channel_kernel, chunk=512, final_norm=final_norm),
        out_shape=jax.ShapeDtypeStruct(x.shape, F32),
        grid=(b, l // tm),
        in_specs=[tok(d), tok(att.shape[2]), tok(f.shape[2]), tok(hz.shape[2]), vec, resident(w_out.shape),
                  vec, vec, vec, _const_spec((1, d)), resident(w1.shape), resident(w2.shape),
                  _const_spec((1, d))],
        out_specs=tok(d),
        compiler_params=_cparams(("parallel", "arbitrary")),
        name="channel_mix",
    )(x, att, f, hz, gate1, w_out, shift, scale, gate2, norm_g, w1, w2, final_g)


def _rope_partner_perm():
    half = AXIS_ROPE_W // 2
    partner, sign = [], []
    for j in range(QK_ROPE_W):
        first = (j % AXIS_ROPE_W) < half
        partner.append(j + half if first else j - half)
        sign.append(-1.0 if first else 1.0)
    return partner, sign


def _rope_slabs(n_tokens, rotary):
    partner, sign = _rope_partner_perm()
    if rotary:
        rows = n_tokens // ROPE_GRID_W
        row = jnp.repeat(jnp.arange(rows, dtype=F32), ROPE_GRID_W)
        col = jnp.tile(jnp.arange(ROPE_GRID_W, dtype=F32), rows)
        inv = ROPE_THETA ** (-jnp.arange(0, AXIS_ROPE_W, 2, dtype=F32) / AXIS_ROPE_W)
        ang = jnp.concatenate([row[:, None] * inv, col[:, None] * inv], axis=-1)
        cos16, sin16 = jnp.cos(ang), jnp.sin(ang)
        half = AXIS_ROPE_W // 2
        cols = jnp.array([(j // AXIS_ROPE_W) * half + j % half for j in range(QK_ROPE_W)])
        cos32 = cos16[:, cols]
        sin32 = sin16[:, cols] * jnp.array(sign, F32)
    else:
        cos32 = jnp.ones((n_tokens, QK_ROPE_W), F32)
        sin32 = jnp.zeros((n_tokens, QK_ROPE_W), F32)
    pad = HEAD_SLAB - QK_NOPE_W - QK_ROPE_W
    cos_t = jnp.concatenate([jnp.ones((n_tokens, QK_NOPE_W), F32), cos32, jnp.zeros((n_tokens, pad), F32)], axis=1)
    sin_t = jnp.concatenate([jnp.zeros((n_tokens, QK_NOPE_W), F32), sin32, jnp.zeros((n_tokens, pad), F32)], axis=1)
    return cos_t, sin_t, cos_t.T, sin_t.T


def _layer_weights(w_in, q_g, kv_g, w_uq, w_ukv):
    partner, _ = _rope_partner_perm()
    partner = jnp.array(partner)
    d = w_in.shape[0]
    pad = HEAD_SLAB - QK_NOPE_W - QK_ROPE_W
    zeros = lambda rows, w: jnp.zeros((rows, w), w_in.dtype)
    kr = w_in[:, OFF_K_ROPE:OFF_FNET]
    w_in_aug = jnp.concatenate([
        w_in[:, :OFF_K_ROPE],
        zeros(d, QK_NOPE_W), kr, zeros(d, pad),
        zeros(d, QK_NOPE_W), kr[:, partner], zeros(d, pad),
        w_in[:, OFF_FNET:]], axis=1)
    wq = w_uq.reshape(Q_LORA_W, HEADS, QK_NOPE_W + QK_ROPE_W)
    wq_a = w_uq
    wq_b = wq[:, :, QK_NOPE_W:][:, :, partner].reshape(Q_LORA_W, HEADS * QK_ROPE_W)
    wkv = w_ukv.reshape(KV_LORA_W, HEADS, QK_NOPE_W + V_HEAD_W)
    wk = jnp.concatenate([wkv[:, :, :QK_NOPE_W], jnp.zeros((KV_LORA_W, HEADS, HEAD_SLAB - QK_NOPE_W), w_ukv.dtype)],
                         axis=2).reshape(KV_LORA_W, HEADS * HEAD_SLAB)
    wv = wkv[:, :, QK_NOPE_W:].reshape(KV_LORA_W, HEADS * V_HEAD_W)
    cast = lambda a: a.astype(MXU_DTYPE)
    return {"w_in": cast(w_in_aug), "q_g": q_g.reshape(1, -1), "kv_g": kv_g.reshape(1, -1),
            "wq_a_t": cast(wq_a.T), "wq_b_t": cast(wq_b.T), "wk": cast(wk), "wv_t": cast(wv.T)}


def _fourier_and_hyena(f, u, hy):
    conv_w, conv_b, d_bias = hy[0], hy[1], hy[8]
    fm = _fourier_mix(f)
    x0, z = _hyena_pre(u, conv_w, conv_b)
    kf = _hyena_spectrum(_hyena_filter_taps(u.shape[1], hy))
    hz = _hyena_gated_conv(x0, z, d_bias, kf)
    return fm, hz


def kernel(x, c, ctx, c_ctx, norm1_g, norm2_g, w_mod, b_mod, w_in, q_norm_g, kv_norm_g, w_uq, w_ukv,
           hy_conv_w, hy_conv_b, hy_w1, hy_b1, hy_freq, hy_w2, hy_b2, hy_w3, hy_d, w_out, w_mlp1,
           w_mlp2, final_norm_g):
    b, l, d = x.shape
    lc = ctx.shape[1]
    depth = w_mod.shape[0]
    rows = 16
    cc = jnp.concatenate([c, c_ctx[None, :], jnp.zeros((rows - b - 1, d), F32)], axis=0)
    mod = _modulation(cc, w_mod, b_mod)
    rope_x = _rope_slabs(l, True)
    rope_c = _rope_slabs(lc, False)
    final_g = final_norm_g.reshape(1, d)
    xc = ctx
    for li in range(depth):
        last = li == depth - 1
        mx = mod[li, :b].reshape(b, 1, N_MODULATION, d)
        mc = jnp.broadcast_to(mod[li, b].reshape(1, 1, N_MODULATION, d), (b, 1, N_MODULATION, d))
        sh1, sc1, g1, sh2, sc2, g2 = [mx[:, :, i] for i in range(N_MODULATION)]
        csh1, csc1, cg1, csh2, csc2, cg2 = [mc[:, :, i] for i in range(N_MODULATION)]
        hy = (hy_conv_w[li], hy_conv_b[li], hy_w1[li], hy_b1[li], hy_freq[li], hy_w2[li], hy_b2[li],
              hy_w3[li], hy_d[li])
        lw = _layer_weights(w_in[li], q_norm_g[li], kv_norm_g[li], w_uq[li], w_ukv[li])
        n1g = norm1_g[li].reshape(1, d)
        n2g = norm2_g[li].reshape(1, d)
        wo = w_out[li].astype(MXU_DTYPE)
        w1 = w_mlp1[li].astype(MXU_DTYPE)
        w2 = w_mlp2[li].astype(MXU_DTYPE)

        q_x, k_x, v_x, f_x, u_x = _in_proj(x, sh1, sc1, n1g, lw, rope_x)
        q_c, k_c, v_c, f_c, u_c = _in_proj(xc, csh1, csc1, n1g, lw, rope_c)
        att_x = _attention(q_x, [(k_x, v_x), (k_c, v_c)])
        fm_x, hz_x = _fourier_and_hyena(f_x, u_x, hy)
        x = _channel_mix(x, att_x, fm_x, hz_x, g1, wo, sh2, sc2, g2, n2g, w1, w2, final_g, last)
        if not last:
            att_c = _attention(q_c, [(k_c, v_c)])
            fm_c, hz_c = _fourier_and_hyena(f_c, u_c, hy)
            xc = _channel_mix(xc, att_c, fm_c, hz_c, cg1, wo, csh2, csc2, cg2, n2g, w1, w2, final_g, False)
    return x
```
